```python
import math
import jax, jax.numpy as jnp
from jax import lax
import numpy as np

D_MODEL = 1024
BATCH = 16
SEQ = 2048
DEPTH = 2

POOL_GROUPS = 4
POOL_WINDOWS = (2, 4, 8, 16)
POOL_WIDTH = D_MODEL // 2
POOL_GROUP_DIM = POOL_WIDTH // POOL_GROUPS
N_HEADS = D_MODEL // 128
HEAD_DIM = 64
V_HEAD_DIM = 2 * HEAD_DIM
QK_WIDTH = N_HEADS * 2 * HEAD_DIM
V_WIDTH = N_HEADS * V_HEAD_DIM
ROPE_THETA = 500000.0
ROT_DIM = HEAD_DIM // 4
Q_BLOCK = 128
N_BRANCHES = 2
IN_WIDTH = POOL_WIDTH + 2 * QK_WIDTH + V_WIDTH + N_BRANCHES * D_MODEL
N_EXPERTS = 32
TOP_K = 4
D_FF = D_MODEL
SWIGLU_ALPHA = 1.702
SWIGLU_LIMIT = 7.0
EXPERT_BLOCK = 128
LN_EPS = 1e-5
DEEPNORM_ALPHA = (2 * DEPTH) ** 0.25
DEEPNORM_BETA = (8 * DEPTH) ** -0.25

kernel_name = 'pool_diffattn_gated_moe_deepnorm'


def layer_norm(x, g, b):
    xf = x.astype(jnp.float32)
    mu = jnp.mean(xf, axis=-1, keepdims=True)
    var = jnp.mean(jnp.square(xf - mu), axis=-1, keepdims=True)
    return ((xf - mu) * lax.rsqrt(var + LN_EPS) * g + b).astype(x.dtype)


def rotary_tables(positions):
    inv_freq = ROPE_THETA ** (-jnp.arange(0, ROT_DIM, 2, dtype=jnp.float32) / ROT_DIM)
    ang = positions.astype(jnp.float32)[..., None] * inv_freq
    return jnp.cos(ang)[:, :, None, None, :], jnp.sin(ang)[:, :, None, None, :]


def partial_rotary(t, cos, sin):
    half = ROT_DIM // 2
    tf = t[..., :ROT_DIM].astype(jnp.float32)
    t1, t2 = tf[..., :half], tf[..., half:]
    rot = jnp.concatenate([t1 * cos - t2 * sin, t2 * cos + t1 * sin], axis=-1)
    return jnp.concatenate([rot.astype(t.dtype), t[..., ROT_DIM:]], axis=-1)


def multiscale_pool(u, w_grp, scale):
    B, S, _ = u.shape
    uf = u.reshape(B, S, POOL_GROUPS, POOL_GROUP_DIM).astype(jnp.float32)
    c = jnp.cumsum(uf, axis=1)
    means = []
    for g, w in enumerate(POOL_WINDOWS):
        cg = c[:, :, g]
        prev = jnp.pad(cg[:, :S - w], ((0, 0), (w, 0), (0, 0)))
        cnt = jnp.minimum(jnp.arange(1, S + 1), w).astype(jnp.float32)[None, :, None]
        means.append((cg - prev) / cnt)
    d = (jnp.stack(means, axis=2) - uf).astype(u.dtype)
    y = jnp.einsum('bsgc,gcd->bsgd', d, w_grp).reshape(B, S, POOL_WIDTH)
    return y * scale


def diff_attention(q, k, v, lam):
    B, S = q.shape[0], q.shape[1]
    nb = S // Q_BLOCK
    qb = q.reshape(B, nb, Q_BLOCK, N_HEADS, 2, HEAD_DIM).transpose(1, 0, 3, 4, 2, 5)
    kt = k.transpose(0, 2, 3, 1, 4)
    vt = v.transpose(0, 2, 1, 3)
    key_pos = jnp.arange(S)
    scale = HEAD_DIM ** -0.5

    def block(args):
        qblk, i = args
        s = jnp.einsum('bhmqd,bhmkd->bhmqk', qblk, kt).astype(jnp.float32) * scale
        qpos = i * Q_BLOCK + jnp.arange(Q_BLOCK)
        mask = key_pos[None, :] <= qpos[:, None]
        p = jax.nn.softmax(jnp.where(mask, s, -jnp.inf), axis=-1)
        a = (p[:, :, 0] - lam * p[:, :, 1]).astype(v.dtype)
        return jnp.einsum('bhqk,bhkv->bhqv', a, vt)

    o = lax.map(block, (qb, jnp.arange(nb)))
    return o.transpose(1, 0, 3, 2, 4).reshape(B, S, N_HEADS, V_HEAD_DIM)


def hybrid_mixer(x, cos, sin, layer, w_in, pool_w, pool_scale, w_pool_branch,
                 w_attn_branch, lq1, lk1, lq2, lk2, subln_w, w_out):
    B, S, D = x.shape
    proj = x @ w_in
    splits = [int(c) for c in np.cumsum([POOL_WIDTH, QK_WIDTH, QK_WIDTH, V_WIDTH])]
    u_pool, q, k, v, gates = jnp.split(proj, splits, axis=-1)
    y_pool = multiscale_pool(u_pool, pool_w, pool_scale) @ w_pool_branch
    q = partial_rotary(q.reshape(B, S, N_HEADS, 2, HEAD_DIM), cos, sin)
    k = partial_rotary(k.reshape(B, S, N_HEADS, 2, HEAD_DIM), cos, sin)
    v = v.reshape(B, S, N_HEADS, V_HEAD_DIM)
    lambda_init = 0.8 - 0.6 * math.exp(-0.3 * layer)
    lam = (jnp.exp(jnp.sum(lq1.astype(jnp.float32) * lk1.astype(jnp.float32)))
           - jnp.exp(jnp.sum(lq2.astype(jnp.float32) * lk2.astype(jnp.float32)))
           + lambda_init)
    o = diff_attention(q, k, v, lam).astype(jnp.float32)
    o = o * lax.rsqrt(jnp.mean(jnp.square(o), axis=-1, keepdims=True) + LN_EPS)
    o = (o * subln_w * (1.0 - lambda_init)).astype(x.dtype)
    y_attn = o.reshape(B, S, V_WIDTH) @ w_attn_branch
    g = jax.nn.sigmoid(gates.astype(jnp.float32)).astype(x.dtype).reshape(B, S, N_BRANCHES, D)
    merged = g[:, :, 0] * y_pool + g[:, :, 1] * y_attn
    return merged @ w_out


def moe_ffn(x2d, w_router, b_router, w_gu, b_gu, w_down, b_down):
    T, D = x2d.shape
    logits = (x2d @ w_router + b_router).astype(jnp.float32)
    top_val, top_idx = lax.top_k(logits, TOP_K)
    gate = jax.nn.softmax(top_val, axis=-1).astype(x2d.dtype)
    A = T * TOP_K
    flat_e = top_idx.reshape(A)
    order = jnp.argsort(flat_e)
    sorted_e = flat_e[order]
    counts = jnp.bincount(flat_e, length=N_EXPERTS)
    padded = (counts + EXPERT_BLOCK - 1) // EXPERT_BLOCK * EXPERT_BLOCK
    start = jnp.cumsum(counts) - counts
    pend = jnp.cumsum(padded)
    pstart = pend - padded
    dest = pstart[sorted_e] + jnp.arange(A) - start[sorted_e]
    n_blocks = -(-A // EXPERT_BLOCK) + N_EXPERTS
    rows = n_blocks * EXPERT_BLOCK
    row_tok = jnp.zeros((rows,), jnp.int32).at[dest].set((order // TOP_K).astype(jnp.int32))
    row_gate = jnp.zeros((rows,), x2d.dtype).at[dest].set(gate.reshape(A)[order])
    block_e = jnp.minimum(jnp.searchsorted(pend, jnp.arange(n_blocks) * EXPERT_BLOCK, side='right'),
                          N_EXPERTS - 1)
    xs = x2d[row_tok].reshape(n_blocks, EXPERT_BLOCK, D)

    def expert_block(args):
        xb, e = args
        h = xb @ w_gu[e] + b_gu[e]
        hg = jnp.minimum(h[:, :D_FF], SWIGLU_LIMIT)
        hu = jnp.clip(h[:, D_FF:], -SWIGLU_LIMIT, SWIGLU_LIMIT)
        act = (hu + 1.0) * hg * jax.nn.sigmoid(SWIGLU_ALPHA * hg)
        return act @ w_down[e] + b_down[e]

    ys = lax.map(expert_block, (xs, block_e)).reshape(rows, D)
    return jnp.zeros_like(x2d).at[row_tok].add(row_gate[:, None] * ys)


def setup_inputs(seed: int = 0) -> dict:
    key = jax.random.key(seed)
    ks = jax.random.split(key, 24)

    def nrm(k, shape, scale):
        return jax.random.normal(k, shape, jnp.float32) * scale

    L, D, E, F = DEPTH, D_MODEL, N_EXPERTS, D_FF
    v_lo = POOL_WIDTH + 2 * QK_WIDTH
    w_in = nrm(ks[2], (L, D, IN_WIDTH), D ** -0.5)
    w_in = w_in.at[:, :, v_lo:v_lo + V_WIDTH].multiply(DEEPNORM_BETA)
    return {
        'x': nrm(ks[0], (BATCH, SEQ, D), 1.0),
        'positions': jnp.broadcast_to(jnp.arange(SEQ, dtype=jnp.int32), (BATCH, SEQ)),
        'w_in': w_in,
        'pool_w': nrm(ks[3], (L, POOL_GROUPS, POOL_GROUP_DIM, POOL_GROUP_DIM), POOL_GROUP_DIM ** -0.5),
        'pool_scale': 1.0 + nrm(ks[4], (L, POOL_WIDTH), 0.02),
        'w_pool_branch': nrm(ks[5], (L, POOL_WIDTH, D), POOL_WIDTH ** -0.5),
        'w_attn_branch': nrm(ks[6], (L, V_WIDTH, D), V_WIDTH ** -0.5),
        'lambda_q1': nrm(ks[7], (L, HEAD_DIM), 0.1),
        'lambda_k1': nrm(ks[8], (L, HEAD_DIM), 0.1),
        'lambda_q2': nrm(ks[9], (L, HEAD_DIM), 0.1),
        'lambda_k2': nrm(ks[10], (L, HEAD_DIM), 0.1),
        'subln_w': 1.0 + nrm(ks[11], (L, V_HEAD_DIM), 0.02),
        'w_out': nrm(ks[12], (L, D, D), D ** -0.5 * DEEPNORM_BETA),
        'ln1_g': 1.0 + nrm(ks[13], (L, D), 0.02),
        'ln1_b': nrm(ks[14], (L, D), 0.02),
        'w_router': nrm(ks[15], (L, D, E), D ** -0.5),
        'b_router': nrm(ks[16], (L, E), 0.01),
        'w_gu': nrm(ks[17], (L, E, D, 2 * F), D ** -0.5 * DEEPNORM_BETA),
        'b_gu': nrm(ks[18], (L, E, 2 * F), 0.02),
        'w_down': nrm(ks[19], (L, E, F, D), F ** -0.5 * DEEPNORM_BETA),
        'b_down': nrm(ks[20], (L, E, D), 0.02),
        'ln2_g': 1.0 + nrm(ks[21], (L, D), 0.02),
        'ln2_b': nrm(ks[22], (L, D), 0.02),
    }


def reference(x, positions, w_in, pool_w, pool_scale, w_pool_branch, w_attn_branch,
              lambda_q1, lambda_k1, lambda_q2, lambda_k2, subln_w, w_out, ln1_g, ln1_b,
              w_router, b_router, w_gu, b_gu, w_down, b_down, ln2_g, ln2_b):
    B, S, D = x.shape
    cos, sin = rotary_tables(positions)
    for l in range(DEPTH):
        mix = hybrid_mixer(x, cos, sin, l, w_in[l], pool_w[l], pool_scale[l], w_pool_branch[l],
                           w_attn_branch[l], lambda_q1[l], lambda_k1[l], lambda_q2[l],
                           lambda_k2[l], subln_w[l], w_out[l])
        x = layer_norm(DEEPNORM_ALPHA * x + mix, ln1_g[l], ln1_b[l])
        ffn = moe_ffn(x.reshape(B * S, D), w_router[l], b_router[l], w_gu[l], b_gu[l],
                      w_down[l], b_down[l]).reshape(B, S, D)
        x = layer_norm(DEEPNORM_ALPHA * x + ffn, ln2_g[l], ln2_b[l])
    return x
```

```python
import functools
import math

import jax
import jax.numpy as jnp
from jax import lax
from jax.experimental import pallas as pl
from jax.experimental.pallas import tpu as pltpu

F32 = jnp.float32
BF16 = jnp.bfloat16

D_MODEL = 1024
DEPTH = 2
POOL_WINDOWS = (2, 4, 8, 16)
POOL_GROUP_DIM = 128
POOL_WIDTH = POOL_GROUP_DIM * len(POOL_WINDOWS)
N_HEADS = 8
HEAD_DIM = 64
V_HEAD_DIM = 2 * HEAD_DIM
QK_WIDTH = N_HEADS * 2 * HEAD_DIM
V_WIDTH = N_HEADS * V_HEAD_DIM
ROPE_THETA = 500000.0
ROT_DIM = HEAD_DIM // 4
N_EXPERTS = 32
TOP_K = 4
D_FF = D_MODEL
SWIGLU_ALPHA = 1.702
SWIGLU_LIMIT = 7.0
LN_EPS = 1e-5
DEEPNORM_ALPHA = (2 * DEPTH) ** 0.25

LANES = 128
VMEM_LIMIT = 48 * 1024 * 1024

PROJ_ROWS = 512
ATTN_BLOCK = 256
MIX_ROWS = 256
EXPERT_ROWS = 256
COMBINE_ROWS = 256
NEG_BIG = -1e30


def _dot(a, b):
    return jnp.dot(a, b, preferred_element_type=F32)


def _layer_norm(z, g, b):
    mu = jnp.mean(z, axis=-1, keepdims=True)
    zc = z - mu
    var = jnp.mean(zc * zc, axis=-1, keepdims=True)
    return zc * lax.rsqrt(var + LN_EPS) * g + b


def _proj_kernel(xb_ref, wu_ref, wqk_ref, wv_ref, c_ref, sa_ref, sb_ref, u_ref, qk_ref, v_ref):
    xb = xb_ref[...]
    u_ref[...] = _dot(xb, wu_ref[...])
    c = c_ref[...]
    sa = sa_ref[...]
    sb = sb_ref[...]
    chunk = 4 * LANES
    for j in range(2 * QK_WIDTH // chunk):
        t = _dot(xb, wqk_ref[:, j * chunk:(j + 1) * chunk])
        for s in range(chunk // LANES):
            ts = t[:, s * LANES:(s + 1) * LANES]
            up = pltpu.roll(ts, LANES - ROT_DIM // 2, axis=1)
            dn = pltpu.roll(ts, ROT_DIM // 2, axis=1)
            lo = j * chunk + s * LANES
            qk_ref[:, lo:lo + LANES] = (ts * c + up * sa + dn * sb).astype(BF16)
    for j in range(V_WIDTH // chunk):
        v_ref[:, j * chunk:(j + 1) * chunk] = _dot(
            xb, wv_ref[:, j * chunk:(j + 1) * chunk]).astype(BF16)


def _projections(xb, wu, wqk, wv, rot_c, rot_sa, rot_sb):
    T = xb.shape[0]
    tm = PROJ_ROWS
    row = lambda i: (i, 0)
    fixed = lambda i: (0, 0)
    return pl.pallas_call(
        _proj_kernel,
        grid=(T // tm,),
        in_specs=[
            pl.BlockSpec((tm, D_MODEL), row),
            pl.BlockSpec((D_MODEL, POOL_WIDTH), fixed),
            pl.BlockSpec((D_MODEL, 2 * QK_WIDTH), fixed),
            pl.BlockSpec((D_MODEL, V_WIDTH), fixed),
            pl.BlockSpec((tm, LANES), row),
            pl.BlockSpec((tm, LANES), row),
            pl.BlockSpec((tm, LANES), row),
        ],
        out_specs=[
            pl.BlockSpec((tm, POOL_WIDTH), row),
            pl.BlockSpec((tm, 2 * QK_WIDTH), row),
            pl.BlockSpec((tm, V_WIDTH), row),
        ],
        out_shape=[
            jax.ShapeDtypeStruct((T, POOL_WIDTH), F32),
            jax.ShapeDtypeStruct((T, 2 * QK_WIDTH), BF16),
            jax.ShapeDtypeStruct((T, V_WIDTH), BF16),
        ],
        compiler_params=pltpu.CompilerParams(
            dimension_semantics=("parallel",), vmem_limit_bytes=VMEM_LIMIT),
        name="projections",
    )(xb, wu, wqk, wv, rot_c, rot_sa, rot_sb)


def _pool_kernel(u_ref, w_ref, sc_ref, p_ref):
    S = u_ref.shape[0]
    row = lax.broadcasted_iota(jnp.int32, (S, POOL_GROUP_DIM), 0)
    for g, window in enumerate(POOL_WINDOWS):
        cols = slice(g * POOL_GROUP_DIM, (g + 1) * POOL_GROUP_DIM)
        u = u_ref[:, cols]
        acc = u
        span = 1
        while span < window:
            shifted = jnp.where(row >= span, pltpu.roll(acc, span, axis=0), 0.0)
            acc = acc + shifted
            span *= 2
        count = jnp.minimum(row + 1, window).astype(F32)
        d = (acc / count - u).astype(BF16)
        y = _dot(d, w_ref[g]) * sc_ref[:, cols]
        p_ref[:, cols] = y.astype(BF16)


def _pool(u, pool_w, pool_scale, B, S):
    T = u.shape[0]
    return pl.pallas_call(
        _pool_kernel,
        grid=(B,),
        in_specs=[
            pl.BlockSpec((S, POOL_WIDTH), lambda b: (b, 0)),
            pl.BlockSpec(pool_w.shape, lambda b: (0, 0, 0)),
            pl.BlockSpec((1, POOL_WIDTH), lambda b: (0, 0)),
        ],
        out_specs=pl.BlockSpec((S, POOL_WIDTH), lambda b: (b, 0)),
        out_shape=jax.ShapeDtypeStruct((T, POOL_WIDTH), BF16),
        compiler_params=pltpu.CompilerParams(
            dimension_semantics=("parallel",), vmem_limit_bytes=VMEM_LIMIT),
        name="pool",
    )(u, pool_w, pool_scale)


def _attn_kernel(lam_ref, q_ref, k_ref, v_ref, w_ref, o_ref, *, post_scale):
    tq = q_ref.shape[0]
    qi = pl.program_id(2)
    q = q_ref[...]
    lane = lax.broadcasted_iota(jnp.int32, q.shape, 1)
    zero = jnp.zeros_like(q)
    q2 = jnp.concatenate(
        [jnp.where(lane < HEAD_DIM, q, zero), jnp.where(lane >= HEAD_DIM, q, zero)], axis=0)

    def step(j, carry, diagonal):
        m, l, acc = carry
        start = pl.multiple_of(j * tq, tq)
        ks = k_ref[pl.ds(start, tq), :]
        vs = v_ref[pl.ds(start, tq), :]
        s = lax.dot_general(q2, ks, (((1,), (1,)), ((), ())), preferred_element_type=F32)
        if diagonal:
            r = lax.broadcasted_iota(jnp.int32, s.shape, 0)
            r = jnp.where(r >= tq, r - tq, r)
            c = lax.broadcasted_iota(jnp.int32, s.shape, 1)
            s = jnp.where(c <= r, s, -jnp.inf)
        m_new = jnp.maximum(m, jnp.max(s, axis=1, keepdims=True))
        rescale = jnp.exp(m - m_new)
        p = jnp.exp(s - m_new)
        l = rescale * l + jnp.sum(p, axis=1, keepdims=True)
        acc = rescale * acc + _dot(p.astype(BF16), vs)
        return m_new, l, acc

    init = (jnp.full((2 * tq, 1), -jnp.inf, F32), jnp.zeros((2 * tq, 1), F32),
            jnp.zeros((2 * tq, V_HEAD_DIM), F32))
    carry = lax.fori_loop(0, qi, lambda j, c: step(j, c, False), init)
    _, l, acc = step(qi, carry, True)
    o = acc / l
    od = o[:tq] - lam_ref[0] * o[tq:]
    od = od * lax.rsqrt(jnp.mean(od * od, axis=-1, keepdims=True) + LN_EPS)
    o_ref[...] = (od * w_ref[...] * post_scale).astype(BF16)


def _attention(qk, v, lam, subln_w, B, S, post_scale):
    T = qk.shape[0]
    tq = ATTN_BLOCK
    nq = S // tq
    return pl.pallas_call(
        functools.partial(_attn_kernel, post_scale=post_scale),
        grid=(B, N_HEADS, nq),
        in_specs=[
            pl.BlockSpec(memory_space=pltpu.SMEM),
            pl.BlockSpec((tq, V_HEAD_DIM), lambda b, h, i: (b * nq + i, h)),
            pl.BlockSpec((S, V_HEAD_DIM), lambda b, h, i: (b, N_HEADS + h)),
            pl.BlockSpec((S, V_HEAD_DIM), lambda b, h, i: (b, h)),
            pl.BlockSpec((1, V_HEAD_DIM), lambda b, h, i: (0, 0)),
        ],
        out_specs=pl.BlockSpec((tq, V_HEAD_DIM), lambda b, h, i: (b * nq + i, h)),
        out_shape=jax.ShapeDtypeStruct((T, V_WIDTH), BF16),
        compiler_params=pltpu.CompilerParams(
            dimension_semantics=("parallel", "parallel", "arbitrary"),
            vmem_limit_bytes=VMEM_LIMIT),
        name="diff_attention",
    )(lam, qk, qk, v, subln_w)


def _mix_kernel(xb_ref, x_ref, p_ref, o_ref, wg_ref, wpb_ref, wab_ref, wout_ref, g_ref, b_ref,
                wrh_ref, wrl_ref, br_ref, x1_ref, idx_ref, gate_ref):
    xb = xb_ref[...]
    merged = jax.nn.sigmoid(_dot(xb, wg_ref[:, :D_MODEL])) * _dot(p_ref[...], wpb_ref[...])
    merged = merged + jax.nn.sigmoid(_dot(xb, wg_ref[:, D_MODEL:])) * _dot(o_ref[...], wab_ref[...])
    mix = _dot(merged.astype(BF16), wout_ref[...])
    x1 = _layer_norm(DEEPNORM_ALPHA * x_ref[...] + mix, g_ref[...], b_ref[...])
    x1_ref[...] = x1

    hi = x1.astype(BF16)
    lo = (x1 - hi.astype(F32)).astype(BF16)
    logits = _dot(hi, wrh_ref[...]) + _dot(lo, wrh_ref[...]) + _dot(hi, wrl_ref[...]) + br_ref[...]

    lane = lax.broadcasted_iota(jnp.int32, logits.shape, 1)
    lane_f = lane.astype(F32)
    work = logits
    vals, idxs = [], []
    for _ in range(TOP_K):
        top = jnp.max(work, axis=1, keepdims=True)
        first = jnp.min(jnp.where(work == top, lane_f, float(LANES)), axis=1, keepdims=True)
        vals.append(top)
        idxs.append(first)
        work = jnp.where(lane_f == first, -jnp.inf, work)
    exps = [jnp.exp(v - vals[0]) for v in vals]
    denom = exps[0] + exps[1] + exps[2] + exps[3]
    idx_out = jnp.zeros(logits.shape, F32)
    gate_out = jnp.zeros(logits.shape, F32)
    for k in range(TOP_K):
        idx_out = jnp.where(lane == k, idxs[k], idx_out)
        gate_out = jnp.where(lane == k, exps[k] / denom, gate_out)
    idx_ref[...] = idx_out.astype(jnp.int32)
    gate_ref[...] = gate_out


def _mix(xb, x, p, o, wg, wpb, wab, wout, ln_g, ln_b, wr_hi, wr_lo, br):
    T = x.shape[0]
    tm = MIX_ROWS
    row = lambda i: (i, 0)
    fixed = lambda i: (0, 0)
    return pl.pallas_call(
        _mix_kernel,
        grid=(T // tm,),
        in_specs=[
            pl.BlockSpec((tm, D_MODEL), row),
            pl.BlockSpec((tm, D_MODEL), row),
            pl.BlockSpec((tm, POOL_WIDTH), row),
            pl.BlockSpec((tm, V_WIDTH), row),
            pl.BlockSpec(wg.shape, fixed),
            pl.BlockSpec(wpb.shape, fixed),
            pl.BlockSpec(wab.shape, fixed),
            pl.BlockSpec(wout.shape, fixed),
            pl.BlockSpec((1, D_MODEL), fixed),
            pl.BlockSpec((1, D_MODEL), fixed),
            pl.BlockSpec((D_MODEL, LANES), fixed),
            pl.BlockSpec((D_MODEL, LANES), fixed),
            pl.BlockSpec((1, LANES), fixed),
        ],
        out_specs=[
            pl.BlockSpec((tm, D_MODEL), row),
            pl.BlockSpec((tm, LANES), row),
            pl.BlockSpec((tm, LANES), row),
        ],
        out_shape=[
            jax.ShapeDtypeStruct((T, D_MODEL), F32),
            jax.ShapeDtypeStruct((T, LANES), jnp.int32),
            jax.ShapeDtypeStruct((T, LANES), F32),
        ],
        compiler_params=pltpu.CompilerParams(
            dimension_semantics=("parallel",), vmem_limit_bytes=VMEM_LIMIT),
        name="mixer_out_router",
    )(xb, x, p, o, wg, wpb, wab, wout, ln_g, ln_b, wr_hi, wr_lo, br)


def _gmm_kernel(be_ref, nu_ref, rt_ref, x_hbm, wgu_ref, bgu_ref, wd_ref, bd_ref, y_ref,
                xbuf, wgu_b, wd_b, sem):
    i = pl.program_id(0)
    rows = xbuf.shape[0]

    @pl.when(i < nu_ref[0])
    def _():
        def issue(r, c):
            tok = rt_ref[0, 0, r]
            pltpu.make_async_copy(x_hbm.at[pl.ds(tok, 1)], xbuf.at[pl.ds(r, 1)], sem).start()
            return c

        lax.fori_loop(0, rows, issue, 0)

        @pl.when((i == 0) | (be_ref[i] != be_ref[jnp.maximum(i - 1, 0)]))
        def _():
            step = 128

            def cast(c, carry):
                sl = pl.ds(pl.multiple_of(c * step, step), step)
                wgu_b[sl, :] = wgu_ref[0, sl, :].astype(BF16)
                wd_b[sl, :] = wd_ref[0, sl, :].astype(BF16)
                return carry

            lax.fori_loop(0, D_MODEL // step, cast, 0)

        pltpu.make_async_copy(x_hbm.at[pl.ds(0, rows)], xbuf, sem).wait()
        h = _dot(xbuf[...].astype(BF16), wgu_b[...]) + bgu_ref[0]
        hg = jnp.minimum(h[:, :D_FF], SWIGLU_LIMIT)
        hu = jnp.clip(h[:, D_FF:], -SWIGLU_LIMIT, SWIGLU_LIMIT)
        act = (hu + 1.0) * hg * jax.nn.sigmoid(SWIGLU_ALPHA * hg)
        y_ref[...] = _dot(act.astype(BF16), wd_b[...]) + bd_ref[0]

    @pl.when(i >= nu_ref[0])
    def _():
        y_ref[...] = jnp.zeros(y_ref.shape, F32)


def _expert_blocks(block_e, n_used, row_tok, x1, w_gu, b_gu, w_down, b_down):
    n_blocks = block_e.shape[0]
    rows = EXPERT_ROWS
    grid_spec = pltpu.PrefetchScalarGridSpec(
        num_scalar_prefetch=2,
        grid=(n_blocks,),
        in_specs=[
            pl.BlockSpec((1, 1, rows), lambda i, be, nu: (i, 0, 0), memory_space=pltpu.SMEM),
            pl.BlockSpec(memory_space=pl.ANY),
            pl.BlockSpec((1, D_MODEL, 2 * D_FF), lambda i, be, nu: (be[i], 0, 0)),
            pl.BlockSpec((1, 1, 2 * D_FF), lambda i, be, nu: (be[i], 0, 0)),
            pl.BlockSpec((1, D_FF, D_MODEL), lambda i, be, nu: (be[i], 0, 0)),
            pl.BlockSpec((1, 1, D_MODEL), lambda i, be, nu: (be[i], 0, 0)),
        ],
        out_specs=pl.BlockSpec((rows, D_MODEL), lambda i, be, nu: (i, 0)),
        scratch_shapes=[
            pltpu.VMEM((rows, D_MODEL), F32),
            pltpu.VMEM((D_MODEL, 2 * D_FF), BF16),
            pltpu.VMEM((D_FF, D_MODEL), BF16),
            pltpu.SemaphoreType.DMA,
        ],
    )
    return pl.pallas_call(
        _gmm_kernel,
        grid_spec=grid_spec,
        out_shape=jax.ShapeDtypeStruct((n_blocks * rows, D_MODEL), F32),
        compiler_params=pltpu.CompilerParams(
            dimension_semantics=("arbitrary",), vmem_limit_bytes=VMEM_LIMIT),
        name="expert_blocks",
    )(block_e, n_used, row_tok.reshape(n_blocks, 1, rows), x1, w_gu,
      b_gu.reshape(N_EXPERTS, 1, 2 * D_FF), w_down, b_down.reshape(N_EXPERTS, 1, D_MODEL))


def _combine_kernel(pos_ref, ys_hbm, gate_ref, x1_ref, g_ref, b_ref, x2_ref, xb2_ref, buf, sem):
    tm = x1_ref.shape[0]

    def issue(r, c):
        for k in range(TOP_K):
            p = pos_ref[0, 0, r * TOP_K + k]
            pltpu.make_async_copy(ys_hbm.at[pl.ds(p, 1)], buf.at[k, pl.ds(r, 1)], sem).start()
        return c

    lax.fori_loop(0, tm, issue, 0)
    gate = gate_ref[...]
    z = DEEPNORM_ALPHA * x1_ref[...]
    for k in range(TOP_K):
        pltpu.make_async_copy(ys_hbm.at[pl.ds(0, tm)], buf.at[k], sem).wait()
    for k in range(TOP_K):
        z = z + gate[:, k:k + 1] * buf[k]
    x2 = _layer_norm(z, g_ref[...], b_ref[...])
    x2_ref[...] = x2
    xb2_ref[...] = x2.astype(BF16)


def _combine(pos, ys, gate, x1, ln_g, ln_b):
    T = x1.shape[0]
    tm = COMBINE_ROWS
    row = lambda i: (i, 0)
    fixed = lambda i: (0, 0)
    return pl.pallas_call(
        _combine_kernel,
        grid=(T // tm,),
        in_specs=[
            pl.BlockSpec((1, 1, tm * TOP_K), lambda i: (i, 0, 0), memory_space=pltpu.SMEM),
            pl.BlockSpec(memory_space=pl.ANY),
            pl.BlockSpec((tm, LANES), row),
            pl.BlockSpec((tm, D_MODEL), row),
            pl.BlockSpec((1, D_MODEL), fixed),
            pl.BlockSpec((1, D_MODEL), fixed),
        ],
        out_specs=[
            pl.BlockSpec((tm, D_MODEL), row),
            pl.BlockSpec((tm, D_MODEL), row),
        ],
        out_shape=[
            jax.ShapeDtypeStruct((T, D_MODEL), F32),
            jax.ShapeDtypeStruct((T, D_MODEL), BF16),
        ],
        scratch_shapes=[
            pltpu.VMEM((TOP_K, tm, D_MODEL), F32),
            pltpu.SemaphoreType.DMA,
        ],
        compiler_params=pltpu.CompilerParams(
            dimension_semantics=("arbitrary",), vmem_limit_bytes=VMEM_LIMIT),
        name="combine_ln",
    )(pos.reshape(T // tm, 1, tm * TOP_K), ys, gate, x1, ln_g, ln_b)


def _routing_tables(top_idx):
    T = top_idx.shape[0]
    A = T * TOP_K
    rows_per = EXPERT_ROWS
    flat_e = top_idx.reshape(A)
    order = jnp.argsort(flat_e, stable=True)
    sorted_e = flat_e[order]
    counts = jnp.bincount(flat_e, length=N_EXPERTS)
    padded = (counts + rows_per - 1) // rows_per * rows_per
    start = jnp.cumsum(counts) - counts
    pend = jnp.cumsum(padded)
    pstart = pend - padded
    dest = (pstart[sorted_e] + jnp.arange(A) - start[sorted_e]).astype(jnp.int32)
    n_blocks = A // rows_per + N_EXPERTS
    row_tok = jnp.zeros((n_blocks * rows_per,), jnp.int32).at[dest].set(
        (order // TOP_K).astype(jnp.int32))
    pos = jnp.zeros((A,), jnp.int32).at[order].set(dest)
    block_e = jnp.minimum(
        jnp.searchsorted(pend, jnp.arange(n_blocks) * rows_per, side='right'),
        N_EXPERTS - 1).astype(jnp.int32)
    n_used = (pend[-1] // rows_per).astype(jnp.int32).reshape(1)
    return row_tok, pos, block_e, n_used


def _rotary_tables(positions):
    half = ROT_DIM // 2
    inv_freq = ROPE_THETA ** (-jnp.arange(0, ROT_DIM, 2, dtype=F32) / ROT_DIM)
    ang = positions.reshape(-1).astype(F32)[:, None] * inv_freq
    cos, sin = jnp.cos(ang), jnp.sin(ang)
    T = ang.shape[0]
    ones = jnp.ones((T, HEAD_DIM - ROT_DIM), F32)
    zeros = jnp.zeros((T, HEAD_DIM - ROT_DIM), F32)
    zh = jnp.zeros((T, half), F32)
    c = jnp.concatenate([cos, cos, ones], axis=1)
    sa = jnp.concatenate([-sin, zh, zeros], axis=1)
    sb = jnp.concatenate([zh, sin, zeros], axis=1)
    reps = LANES // HEAD_DIM
    return jnp.tile(c, (1, reps)), jnp.tile(sa, (1, reps)), jnp.tile(sb, (1, reps))


def kernel(x, positions, w_in, pool_w, pool_scale, w_pool_branch, w_attn_branch, lambda_q1, lambda_k1, lambda_q2, lambda_k2, subln_w, w_out, ln1_g, ln1_b, w_router, b_router, w_gu, b_gu, w_down, b_down, ln2_g, ln2_b):
    B, S, D = x.shape
    assert D == D_MODEL and S % ATTN_BLOCK == 0 and w_in.shape[0] == DEPTH
    T = B * S
    assert T % PROJ_ROWS == 0 and T % MIX_ROWS == 0 and T % COMBINE_ROWS == 0
    assert (T * TOP_K) % EXPERT_ROWS == 0
    rot_c, rot_sa, rot_sb = _rotary_tables(positions)
    xf = x.reshape(T, D)
    xb = xf.astype(BF16)
    o_qk = POOL_WIDTH
    o_v = o_qk + 2 * QK_WIDTH
    o_g = o_v + V_WIDTH
    for l in range(DEPTH):
        wl = w_in[l]
        wu = wl[:, :o_qk].astype(BF16)
        wq = wl[:, o_qk:o_qk + QK_WIDTH] * (HEAD_DIM ** -0.5)
        wqk = jnp.concatenate([wq, wl[:, o_qk + QK_WIDTH:o_v]], axis=1).astype(BF16)
        wv = wl[:, o_v:o_g].astype(BF16)
        wg = wl[:, o_g:].astype(BF16)
        u, qk, v = _projections(xb, wu, wqk, wv, rot_c, rot_sa, rot_sb)
        p = _pool(u, pool_w[l].astype(BF16), pool_scale[l].reshape(1, POOL_WIDTH), B, S)
        lambda_init = 0.8 - 0.6 * math.exp(-0.3 * l)
        lam = (jnp.exp(jnp.sum(lambda_q1[l] * lambda_k1[l]))
               - jnp.exp(jnp.sum(lambda_q2[l] * lambda_k2[l])) + lambda_init).reshape(1)
        o = _attention(qk, v, lam, subln_w[l].reshape(1, V_HEAD_DIM), B, S, 1.0 - lambda_init)
        wr = jnp.zeros((D, LANES), F32).at[:, :N_EXPERTS].set(w_router[l])
        wr_hi = wr.astype(BF16)
        wr_lo = (wr - wr_hi.astype(F32)).astype(BF16)
        br = jnp.full((1, LANES), NEG_BIG, F32).at[0, :N_EXPERTS].set(b_router[l])
        x1, idx, gate = _mix(
            xb, xf, p, o, wg, w_pool_branch[l].astype(BF16), w_attn_branch[l].astype(BF16),
            w_out[l].astype(BF16), ln1_g[l].reshape(1, D), ln1_b[l].reshape(1, D),
            wr_hi, wr_lo, br)
        row_tok, pos, block_e, n_used = _routing_tables(idx[:, :TOP_K])
        ys = _expert_blocks(block_e, n_used, row_tok, x1, w_gu[l], b_gu[l], w_down[l], b_down[l])
        xf, xb = _combine(pos, ys, gate, x1, ln2_g[l].reshape(1, D), ln2_b[l].reshape(1, D))
    return xf.reshape(B, S, D)
```

```python
import functools
import math

import jax
import jax.numpy as jnp
from jax import lax
from jax.experimental import pallas as pl
from jax.experimental.pallas import tpu as pltpu

F32 = jnp.float32
BF16 = jnp.bfloat16

D_MODEL = 1024
DEPTH = 2
POOL_WINDOWS = (2, 4, 8, 16)
POOL_GROUP_DIM = 128
POOL_WIDTH = POOL_GROUP_DIM * len(POOL_WINDOWS)
N_HEADS = 8
HEAD_DIM = 64
V_HEAD_DIM = 2 * HEAD_DIM
QK_WIDTH = N_HEADS * 2 * HEAD_DIM
V_WIDTH = N_HEADS * V_HEAD_DIM
ROPE_THETA = 500000.0
ROT_DIM = HEAD_DIM // 4
N_EXPERTS = 32
TOP_K = 4
D_FF = D_MODEL
SWIGLU_ALPHA = 1.702
SWIGLU_LIMIT = 7.0
LN_EPS = 1e-5
DEEPNORM_ALPHA = (2 * DEPTH) ** 0.25

LANES = 128
VMEM_LIMIT = 48 * 1024 * 1024

PROJ_ROWS = 512
ATTN_BLOCK = 512
MIX_ROWS = 256
EXPERT_ROWS = 256
COMBINE_ROWS = 256
NEG_BIG = -1e30


def _dot(a, b):
    return jnp.dot(a, b, preferred_element_type=F32)


def _layer_norm(z, g, b):
    mu = jnp.mean(z, axis=-1, keepdims=True)
    zc = z - mu
    var = jnp.mean(zc * zc, axis=-1, keepdims=True)
    return zc * lax.rsqrt(var + LN_EPS) * g + b


def _proj_kernel(xb_ref, wu_ref, wqk_ref, wv_ref, c_ref, sa_ref, sb_ref, u_ref, qk_ref, v_ref):
    xb = xb_ref[...]
    u_ref[...] = _dot(xb, wu_ref[...])
    c = c_ref[...]
    sa = sa_ref[...]
    sb = sb_ref[...]
    chunk = 4 * LANES
    for j in range(2 * QK_WIDTH // chunk):
        t = _dot(xb, wqk_ref[:, j * chunk:(j + 1) * chunk])
        for s in range(chunk // LANES):
            ts = t[:, s * LANES:(s + 1) * LANES]
            up = pltpu.roll(ts, LANES - ROT_DIM // 2, axis=1)
            dn = pltpu.roll(ts, ROT_DIM // 2, axis=1)
            lo = j * chunk + s * LANES
            qk_ref[:, lo:lo + LANES] = (ts * c + up * sa + dn * sb).astype(BF16)
    for j in range(V_WIDTH // chunk):
        v_ref[:, j * chunk:(j + 1) * chunk] = _dot(
            xb, wv_ref[:, j * chunk:(j + 1) * chunk]).astype(BF16)


def _projections(xb, wu, wqk, wv, rot_c, rot_sa, rot_sb):
    T = xb.shape[0]
    tm = PROJ_ROWS
    row = lambda i: (i, 0)
    fixed = lambda i: (0, 0)
    return pl.pallas_call(
        _proj_kernel,
        grid=(T // tm,),
        in_specs=[
            pl.BlockSpec((tm, D_MODEL), row),
            pl.BlockSpec((D_MODEL, POOL_WIDTH), fixed),
            pl.BlockSpec((D_MODEL, 2 * QK_WIDTH), fixed),
            pl.BlockSpec((D_MODEL, V_WIDTH), fixed),
            pl.BlockSpec((tm, LANES), row),
            pl.BlockSpec((tm, LANES), row),
            pl.BlockSpec((tm, LANES), row),
        ],
        out_specs=[
            pl.BlockSpec((tm, POOL_WIDTH), row),
            pl.BlockSpec((tm, 2 * QK_WIDTH), row),
            pl.BlockSpec((tm, V_WIDTH), row),
        ],
        out_shape=[
            jax.ShapeDtypeStruct((T, POOL_WIDTH), F32),
            jax.ShapeDtypeStruct((T, 2 * QK_WIDTH), BF16),
            jax.ShapeDtypeStruct((T, V_WIDTH), BF16),
        ],
        compiler_params=pltpu.CompilerParams(
            dimension_semantics=("parallel",), vmem_limit_bytes=VMEM_LIMIT),
        name="projections",
    )(xb, wu, wqk, wv, rot_c, rot_sa, rot_sb)


def _pool_kernel(u_ref, w_ref, sc_ref, p_ref):
    S = u_ref.shape[0]
    row = lax.broadcasted_iota(jnp.int32, (S, POOL_GROUP_DIM), 0)
    for g, window in enumerate(POOL_WINDOWS):
        cols = slice(g * POOL_GROUP_DIM, (g + 1) * POOL_GROUP_DIM)
        u = u_ref[:, cols]
        acc = u
        span = 1
        while span < window:
            shifted = jnp.where(row >= span, pltpu.roll(acc, span, axis=0), 0.0)
            acc = acc + shifted
            span *= 2
        count = jnp.minimum(row + 1, window).astype(F32)
        d = (acc / count - u).astype(BF16)
        y = _dot(d, w_ref[g]) * sc_ref[:, cols]
        p_ref[:, cols] = y.astype(BF16)


def _pool(u, pool_w, pool_scale, B, S):
    T = u.shape[0]
    return pl.pallas_call(
        _pool_kernel,
        grid=(B,),
        in_specs=[
            pl.BlockSpec((S, POOL_WIDTH), lambda b: (b, 0)),
            pl.BlockSpec(pool_w.shape, lambda b: (0, 0, 0)),
            pl.BlockSpec((1, POOL_WIDTH), lambda b: (0, 0)),
        ],
        out_specs=pl.BlockSpec((S, POOL_WIDTH), lambda b: (b, 0)),
        out_shape=jax.ShapeDtypeStruct((T, POOL_WIDTH), BF16),
        compiler_params=pltpu.CompilerParams(
            dimension_semantics=("parallel",), vmem_limit_bytes=VMEM_LIMIT),
        name="pool",
    )(u, pool_w, pool_scale)


def _attn_kernel(lam_ref, q_ref, k_ref, v_ref, w_ref, o_ref, vt_ref, *, post_scale):
    tq = q_ref.shape[0]
    qi = pl.program_id(2)

    @pl.when(qi == 0)
    def _():
        for c in range(vt_ref.shape[0]):
            vt_ref[c] = v_ref[c * tq:(c + 1) * tq, :].astype(F32).T.astype(BF16)

    q = q_ref[...]
    lane = lax.broadcasted_iota(jnp.int32, q.shape, 1)
    zero = jnp.zeros_like(q)
    q2 = jnp.concatenate(
        [jnp.where(lane < HEAD_DIM, q, zero), jnp.where(lane >= HEAD_DIM, q, zero)], axis=0)

    def step(j, carry, diagonal):
        m, l, acc = carry
        ks = k_ref[pl.ds(pl.multiple_of(j * tq, tq), tq), :]
        s = lax.dot_general(ks, q2, (((1,), (1,)), ((), ())), preferred_element_type=F32)
        if diagonal:
            key = lax.broadcasted_iota(jnp.int32, s.shape, 0)
            qry = lax.broadcasted_iota(jnp.int32, s.shape, 1)
            qry = jnp.where(qry >= tq, qry - tq, qry)
            s = jnp.where(key <= qry, s, -jnp.inf)
        m_new = jnp.maximum(m, jnp.max(s, axis=0, keepdims=True))
        rescale = jnp.exp(m - m_new)
        p = jnp.exp(s - m_new)
        l = rescale * l + jnp.sum(p, axis=0, keepdims=True)
        acc = rescale * acc + _dot(vt_ref[j], p.astype(BF16))
        return m_new, l, acc

    init = (jnp.full((1, 2 * tq), -jnp.inf, F32), jnp.zeros((1, 2 * tq), F32),
            jnp.zeros((V_HEAD_DIM, 2 * tq), F32))
    carry = lax.fori_loop(0, qi, lambda j, c: step(j, c, False), init)
    _, l, acc = step(qi, carry, True)
    o = acc / l
    od = (o[:, :tq] - lam_ref[0] * o[:, tq:]).T
    od = od * lax.rsqrt(jnp.mean(od * od, axis=-1, keepdims=True) + LN_EPS)
    o_ref[...] = (od * w_ref[...] * post_scale).astype(BF16)


def _attention(qk, v, lam, subln_w, B, S, post_scale):
    T = qk.shape[0]
    tq = ATTN_BLOCK
    nq = S // tq
    return pl.pallas_call(
        functools.partial(_attn_kernel, post_scale=post_scale),
        grid=(B, N_HEADS, nq),
        in_specs=[
            pl.BlockSpec(memory_space=pltpu.SMEM),
            pl.BlockSpec((tq, V_HEAD_DIM), lambda b, h, i: (b * nq + i, h)),
            pl.BlockSpec((S, V_HEAD_DIM), lambda b, h, i: (b, N_HEADS + h)),
            pl.BlockSpec((S, V_HEAD_DIM), lambda b, h, i: (b, h)),
            pl.BlockSpec((1, V_HEAD_DIM), lambda b, h, i: (0, 0)),
        ],
        out_specs=pl.BlockSpec((tq, V_HEAD_DIM), lambda b, h, i: (b * nq + i, h)),
        out_shape=jax.ShapeDtypeStruct((T, V_WIDTH), BF16),
        scratch_shapes=[pltpu.VMEM((nq, V_HEAD_DIM, tq), BF16)],
        compiler_params=pltpu.CompilerParams(
            dimension_semantics=("parallel", "parallel", "arbitrary"),
            vmem_limit_bytes=VMEM_LIMIT),
        name="diff_attention",
    )(lam, qk, qk, v, subln_w)


def _mix_kernel(xb_ref, x_ref, p_ref, o_ref, wg_ref, wpb_ref, wab_ref, wout_ref, g_ref, b_ref,
                wrh_ref, wrl_ref, br_ref, x1_ref, idx_ref, gate_ref):
    xb = xb_ref[...]
    merged = jax.nn.sigmoid(_dot(xb, wg_ref[:, :D_MODEL])) * _dot(p_ref[...], wpb_ref[...])
    merged = merged + jax.nn.sigmoid(_dot(xb, wg_ref[:, D_MODEL:])) * _dot(o_ref[...], wab_ref[...])
    mix = _dot(merged.astype(BF16), wout_ref[...])
    x1 = _layer_norm(DEEPNORM_ALPHA * x_ref[...] + mix, g_ref[...], b_ref[...])
    x1_ref[...] = x1

    hi = x1.astype(BF16)
    lo = (x1 - hi.astype(F32)).astype(BF16)
    logits = _dot(hi, wrh_ref[...]) + _dot(lo, wrh_ref[...]) + _dot(hi, wrl_ref[...]) + br_ref[...]

    lane = lax.broadcasted_iota(jnp.int32, logits.shape, 1)
    lane_f = lane.astype(F32)
    work = logits
    vals, idxs = [], []
    for _ in range(TOP_K):
        top = jnp.max(work, axis=1, keepdims=True)
        first = jnp.min(jnp.where(work == top, lane_f, float(LANES)), axis=1, keepdims=True)
        vals.append(top)
        idxs.append(first)
        work = jnp.where(lane_f == first, -jnp.inf, work)
    exps = [jnp.exp(v - vals[0]) for v in vals]
    denom = exps[0] + exps[1] + exps[2] + exps[3]
    idx_out = jnp.zeros(logits.shape, F32)
    gate_out = jnp.zeros(logits.shape, F32)
    for k in range(TOP_K):
        idx_out = jnp.where(lane == k, idxs[k], idx_out)
        gate_out = jnp.where(lane == k, exps[k] / denom, gate_out)
    idx_ref[...] = idx_out.astype(jnp.int32)
    gate_ref[...] = gate_out


def _mix(xb, x, p, o, wg, wpb, wab, wout, ln_g, ln_b, wr_hi, wr_lo, br):
    T = x.shape[0]
    tm = MIX_ROWS
    row = lambda i: (i, 0)
    fixed = lambda i: (0, 0)
    return pl.pallas_call(
        _mix_kernel,
        grid=(T // tm,),
        in_specs=[
            pl.BlockSpec((tm, D_MODEL), row),
            pl.BlockSpec((tm, D_MODEL), row),
            pl.BlockSpec((tm, POOL_WIDTH), row),
            pl.BlockSpec((tm, V_WIDTH), row),
            pl.BlockSpec(wg.shape, fixed),
            pl.BlockSpec(wpb.shape, fixed),
            pl.BlockSpec(wab.shape, fixed),
            pl.BlockSpec(wout.shape, fixed),
            pl.BlockSpec((1, D_MODEL), fixed),
            pl.BlockSpec((1, D_MODEL), fixed),
            pl.BlockSpec((D_MODEL, LANES), fixed),
            pl.BlockSpec((D_MODEL, LANES), fixed),
            pl.BlockSpec((1, LANES), fixed),
        ],
        out_specs=[
            pl.BlockSpec((tm, D_MODEL), row),
            pl.BlockSpec((tm, LANES), row),
            pl.BlockSpec((tm, LANES), row),
        ],
        out_shape=[
            jax.ShapeDtypeStruct((T, D_MODEL), F32),
            jax.ShapeDtypeStruct((T, LANES), jnp.int32),
            jax.ShapeDtypeStruct((T, LANES), F32),
        ],
        compiler_params=pltpu.CompilerParams(
            dimension_semantics=("parallel",), vmem_limit_bytes=VMEM_LIMIT),
        name="mixer_out_router",
    )(xb, x, p, o, wg, wpb, wab, wout, ln_g, ln_b, wr_hi, wr_lo, br)


def _row_copy(src, dst, src_row, dst_row, sem):
    return pltpu.make_async_copy(src.at[pl.ds(src_row, 1)], dst.at[pl.ds(dst_row, 1)], sem)


def _gmm_kernel(be_ref, nu_ref, src_cur, src_nxt, dst_prv, x_hbm, wgu_ref, bgu_ref, wd_ref,
                bd_ref, y_hbm, xbuf, ybuf, wgu_b, wd_b, gsem, ssem):
    i = pl.program_id(0)
    n_used = nu_ref[0]
    rows = xbuf.shape[1]
    cur = i % 2
    nxt = 1 - cur
    n_chunks = 4
    cw = D_FF // n_chunks
    per_chunk = rows // n_chunks

    def wait_gather(slot, count):
        pltpu.make_async_copy(x_hbm.at[pl.ds(0, count)], xbuf.at[slot, pl.ds(0, count)],
                              gsem.at[slot]).wait()

    def wait_scatter(count):
        pltpu.make_async_copy(ybuf.at[nxt, pl.ds(0, count)], y_hbm.at[pl.ds(0, count)],
                              ssem).wait()

    @pl.when(i == 0)
    def _():
        def issue(r, c):
            _row_copy(x_hbm, xbuf.at[0], src_cur[0, 0, r], r, gsem.at[0]).start()
            return c

        lax.fori_loop(0, rows, issue, 0)
        wait_gather(0, rows - per_chunk)
        ybuf[1] = jnp.zeros(ybuf.shape[1:], F32)

    @pl.when(i <= n_used)
    def _():
        wait_gather(cur, per_chunk)

    @pl.when((i < n_used) & ((i == 0) | (be_ref[i] != be_ref[jnp.maximum(i - 1, 0)])))
    def _():
        step = 128

        def cast(c, carry):
            sl = pl.ds(pl.multiple_of(c * step, step), step)
            wgu_b[sl, :] = wgu_ref[sl, :].astype(BF16)
            wd_b[sl, :] = wd_ref[sl, :].astype(BF16)
            return carry

        lax.fori_loop(0, D_MODEL // step, cast, 0)

    @pl.when(i < n_used)
    def _():
        xb = xbuf[cur].astype(BF16)
        y = jnp.zeros((rows, D_MODEL), F32) + bd_ref[...]
        for c in range(n_chunks):
            for r in range(c * per_chunk, (c + 1) * per_chunk):
                _row_copy(x_hbm, xbuf.at[nxt], src_nxt[0, 0, r], r, gsem.at[nxt]).start()
                _row_copy(ybuf.at[nxt], y_hbm, r, dst_prv[0, 0, r], ssem).start()
            g_cols = slice(c * cw, (c + 1) * cw)
            u_cols = slice(D_FF + c * cw, D_FF + (c + 1) * cw)
            hg = _dot(xb, wgu_b[:, g_cols]) + bgu_ref[:, g_cols]
            hu = _dot(xb, wgu_b[:, u_cols]) + bgu_ref[:, u_cols]
            hg = jnp.minimum(hg, SWIGLU_LIMIT)
            hu = jnp.clip(hu, -SWIGLU_LIMIT, SWIGLU_LIMIT)
            act = (hu + 1.0) * hg * jax.nn.sigmoid(SWIGLU_ALPHA * hg)
            y = y + _dot(act.astype(BF16), wd_b[g_cols, :])
            if c >= 1:
                wait_gather(nxt, per_chunk)
                wait_scatter(per_chunk)
        ybuf[cur] = y
        wait_scatter(per_chunk)

    @pl.when(i == n_used)
    def _():
        def issue(r, c):
            _row_copy(ybuf.at[nxt], y_hbm, r, dst_prv[0, 0, r], ssem).start()
            return c

        lax.fori_loop(0, rows, issue, 0)
        wait_scatter(rows)


def _expert_blocks(layer, block_e, n_used, src, dst, x1, w_gu, b_gu, w_down, b_down):
    T = x1.shape[0]
    rows = EXPERT_ROWS
    steps = block_e.shape[0]
    depth = w_gu.shape[0]
    grid_spec = pltpu.PrefetchScalarGridSpec(
        num_scalar_prefetch=2,
        grid=(steps,),
        in_specs=[
            pl.BlockSpec((1, 1, rows), lambda i, be, nu: (i, 0, 0), memory_space=pltpu.SMEM),
            pl.BlockSpec((1, 1, rows), lambda i, be, nu: (jnp.minimum(i + 1, steps - 1), 0, 0),
                         memory_space=pltpu.SMEM),
            pl.BlockSpec((1, 1, rows), lambda i, be, nu: (i, 0, 0), memory_space=pltpu.SMEM),
            pl.BlockSpec(memory_space=pl.ANY),
            pl.BlockSpec((None, None, D_MODEL, 2 * D_FF), lambda i, be, nu: (layer, be[i], 0, 0)),
            pl.BlockSpec((None, None, 1, 2 * D_FF), lambda i, be, nu: (layer, be[i], 0, 0)),
            pl.BlockSpec((None, None, D_FF, D_MODEL), lambda i, be, nu: (layer, be[i], 0, 0)),
            pl.BlockSpec((None, None, 1, D_MODEL), lambda i, be, nu: (layer, be[i], 0, 0)),
        ],
        out_specs=pl.BlockSpec(memory_space=pl.ANY),
        scratch_shapes=[
            pltpu.VMEM((2, rows, D_MODEL), F32),
            pltpu.VMEM((2, rows, D_MODEL), F32),
            pltpu.VMEM((D_MODEL, 2 * D_FF), BF16),
            pltpu.VMEM((D_FF, D_MODEL), BF16),
            pltpu.SemaphoreType.DMA((2,)),
            pltpu.SemaphoreType.DMA,
        ],
    )
    return pl.pallas_call(
        _gmm_kernel,
        grid_spec=grid_spec,
        out_shape=jax.ShapeDtypeStruct((TOP_K * T + rows, D_MODEL), F32),
        compiler_params=pltpu.CompilerParams(
            dimension_semantics=("arbitrary",), vmem_limit_bytes=VMEM_LIMIT),
        name="expert_blocks",
    )(block_e, n_used, src.reshape(steps, 1, rows), src.reshape(steps, 1, rows),
      dst.reshape(steps, 1, rows), x1, w_gu,
      b_gu.reshape(depth, N_EXPERTS, 1, 2 * D_FF), w_down,
      b_down.reshape(depth, N_EXPERTS, 1, D_MODEL))


def _combine_kernel(y0_ref, y1_ref, y2_ref, y3_ref, gate_ref, x1_ref, g_ref, b_ref, x2_ref, xb2_ref):
    gate = gate_ref[...]
    z = DEEPNORM_ALPHA * x1_ref[...]
    for k, y_ref in enumerate((y0_ref, y1_ref, y2_ref, y3_ref)):
        z = z + gate[:, k:k + 1] * y_ref[...]
    x2 = _layer_norm(z, g_ref[...], b_ref[...])
    x2_ref[...] = x2
    xb2_ref[...] = x2.astype(BF16)


def _combine(ys, gate, x1, ln_g, ln_b):
    T = x1.shape[0]
    tm = COMBINE_ROWS
    nt = T // tm
    row = lambda i: (i, 0)
    fixed = lambda i: (0, 0)
    y_specs = [pl.BlockSpec((tm, D_MODEL), functools.partial(lambda k, i: (k * nt + i, 0), k))
               for k in range(TOP_K)]
    return pl.pallas_call(
        _combine_kernel,
        grid=(nt,),
        in_specs=y_specs + [
            pl.BlockSpec((tm, LANES), row),
            pl.BlockSpec((tm, D_MODEL), row),
            pl.BlockSpec((1, D_MODEL), fixed),
            pl.BlockSpec((1, D_MODEL), fixed),
        ],
        out_specs=[
            pl.BlockSpec((tm, D_MODEL), row),
            pl.BlockSpec((tm, D_MODEL), row),
        ],
        out_shape=[
            jax.ShapeDtypeStruct((T, D_MODEL), F32),
            jax.ShapeDtypeStruct((T, D_MODEL), BF16),
        ],
        compiler_params=pltpu.CompilerParams(
            dimension_semantics=("parallel",), vmem_limit_bytes=VMEM_LIMIT),
        name="combine_ln",
    )(ys, ys, ys, ys, gate, x1, ln_g, ln_b)


def _routing_tables(top_idx):
    T = top_idx.shape[0]
    A = T * TOP_K
    rows_per = EXPERT_ROWS
    flat_e = top_idx.reshape(A)
    key = jnp.sort(flat_e * A + jnp.arange(A, dtype=jnp.int32))
    order = key % A
    experts = jnp.arange(N_EXPERTS, dtype=jnp.int32)
    counts = jnp.sum((flat_e[:, None] == experts[None, :]).astype(jnp.int32), axis=0)
    padded = (counts + rows_per - 1) // rows_per * rows_per
    start = jnp.cumsum(counts) - counts
    pend = jnp.cumsum(padded)
    pstart = pend - padded
    steps = A // rows_per + N_EXPERTS + 1
    first_row = jnp.arange(steps, dtype=jnp.int32) * rows_per
    block_e = jnp.minimum(
        jnp.sum((pend[None, :] <= first_row[:, None]).astype(jnp.int32), axis=1), N_EXPERTS - 1)
    r = jnp.arange(steps * rows_per, dtype=jnp.int32)
    e_r = jnp.repeat(block_e, rows_per)
    off = r - pstart[e_r]
    valid = off < counts[e_r]
    a = order[jnp.clip(start[e_r] + off, 0, A - 1)]
    tok = a // TOP_K
    src = jnp.where(valid, tok, 0).astype(jnp.int32)
    spare = TOP_K * T + r % rows_per
    dst = jnp.where(valid, (a % TOP_K) * T + tok, spare).astype(jnp.int32)
    dst = jnp.concatenate([spare[:rows_per], dst[:-rows_per]]).astype(jnp.int32)
    n_used = (pend[-1] // rows_per).astype(jnp.int32).reshape(1)
    return src, dst, block_e.astype(jnp.int32), n_used


def _rotary_tables(positions):
    half = ROT_DIM // 2
    inv_freq = ROPE_THETA ** (-jnp.arange(0, ROT_DIM, 2, dtype=F32) / ROT_DIM)
    ang = positions.reshape(-1).astype(F32)[:, None] * inv_freq
    cos, sin = jnp.cos(ang), jnp.sin(ang)
    T = ang.shape[0]
    ones = jnp.ones((T, HEAD_DIM - ROT_DIM), F32)
    zeros = jnp.zeros((T, HEAD_DIM - ROT_DIM), F32)
    zh = jnp.zeros((T, half), F32)
    c = jnp.concatenate([cos, cos, ones], axis=1)
    sa = jnp.concatenate([-sin, zh, zeros], axis=1)
    sb = jnp.concatenate([zh, sin, zeros], axis=1)
    reps = LANES // HEAD_DIM
    return jnp.tile(c, (1, reps)), jnp.tile(sa, (1, reps)), jnp.tile(sb, (1, reps))


def kernel(x, positions, w_in, pool_w, pool_scale, w_pool_branch, w_attn_branch, lambda_q1, lambda_k1, lambda_q2, lambda_k2, subln_w, w_out, ln1_g, ln1_b, w_router, b_router, w_gu, b_gu, w_down, b_down, ln2_g, ln2_b):
    B, S, D = x.shape
    assert D == D_MODEL and S % ATTN_BLOCK == 0 and w_in.shape[0] == DEPTH
    T = B * S
    assert T % PROJ_ROWS == 0 and T % MIX_ROWS == 0 and T % COMBINE_ROWS == 0
    assert (T * TOP_K) % EXPERT_ROWS == 0
    rot_c, rot_sa, rot_sb = _rotary_tables(positions)
    xf = x.reshape(T, D)
    xb = xf.astype(BF16)
    o_qk = POOL_WIDTH
    o_v = o_qk + 2 * QK_WIDTH
    o_g = o_v + V_WIDTH
    for l in range(DEPTH):
        wl = w_in[l]
        wu = wl[:, :o_qk].astype(BF16)
        wq = wl[:, o_qk:o_qk + QK_WIDTH] * (HEAD_DIM ** -0.5)
        wqk = jnp.concatenate([wq, wl[:, o_qk + QK_WIDTH:o_v]], axis=1).astype(BF16)
        wv = wl[:, o_v:o_g].astype(BF16)
        wg = wl[:, o_g:].astype(BF16)
        u, qk, v = _projections(xb, wu, wqk, wv, rot_c, rot_sa, rot_sb)
        p = _pool(u, pool_w[l].astype(BF16), pool_scale[l].reshape(1, POOL_WIDTH), B, S)
        lambda_init = 0.8 - 0.6 * math.exp(-0.3 * l)
        lam = (jnp.exp(jnp.sum(lambda_q1[l] * lambda_k1[l]))
               - jnp.exp(jnp.sum(lambda_q2[l] * lambda_k2[l])) + lambda_init).reshape(1)
        o = _attention(qk, v, lam, subln_w[l].reshape(1, V_HEAD_DIM), B, S, 1.0 - lambda_init)
        wr = jnp.zeros((D, LANES), F32).at[:, :N_EXPERTS].set(w_router[l])
        wr_hi = wr.astype(BF16)
        wr_lo = (wr - wr_hi.astype(F32)).astype(BF16)
        br = jnp.full((1, LANES), NEG_BIG, F32).at[0, :N_EXPERTS].set(b_router[l])
        x1, idx, gate = _mix(
            xb, xf, p, o, wg, w_pool_branch[l].astype(BF16), w_attn_branch[l].astype(BF16),
            w_out[l].astype(BF16), ln1_g[l].reshape(1, D), ln1_b[l].reshape(1, D),
            wr_hi, wr_lo, br)
        src, dst, block_e, n_used = _routing_tables(idx[:, :TOP_K])
        ys = _expert_blocks(l, block_e, n_used, src, dst, x1, w_gu, b_gu, w_down, b_down)
        xf, xb = _combine(ys, gate, x1, ln2_g[l].reshape(1, D), ln2_b[l].reshape(1, D))
    return xf.reshape(B, S, D)
```

```python
import functools
import math

import jax
import jax.numpy as jnp
from jax import lax
from jax.experimental import pallas as pl
from jax.experimental.pallas import tpu as pltpu

F32 = jnp.float32
BF16 = jnp.bfloat16

D_MODEL = 1024
DEPTH = 2
POOL_WINDOWS = (2, 4, 8, 16)
POOL_GROUP_DIM = 128
POOL_WIDTH = POOL_GROUP_DIM * len(POOL_WINDOWS)
N_HEADS = 8
HEAD_DIM = 64
V_HEAD_DIM = 2 * HEAD_DIM
QK_WIDTH = N_HEADS * 2 * HEAD_DIM
V_WIDTH = N_HEADS * V_HEAD_DIM
ROPE_THETA = 500000.0
ROT_DIM = HEAD_DIM // 4
N_EXPERTS = 32
TOP_K = 4
D_FF = D_MODEL
SWIGLU_ALPHA = 1.702
SWIGLU_LIMIT = 7.0
LN_EPS = 1e-5
DEEPNORM_ALPHA = (2 * DEPTH) ** 0.25

LANES = 128
VMEM_LIMIT = 48 * 1024 * 1024

PROJ_ROWS = 512
ATTN_BLOCK = 512
MIX_ROWS = 256
EXPERT_ROWS = 256
COMBINE_ROWS = 256
NEG_BIG = -1e30


def _dot(a, b):
    return jnp.dot(a, b, preferred_element_type=F32)


def _layer_norm(z, g, b):
    mu = jnp.mean(z, axis=-1, keepdims=True)
    zc = z - mu
    var = jnp.mean(zc * zc, axis=-1, keepdims=True)
    return zc * lax.rsqrt(var + LN_EPS) * g + b


TILE_ROWS = D_MODEL // LANES


def _load_token_tiles(ref, tokens):
    return jnp.concatenate(
        [ref[pl.ds(s, tokens, stride=TILE_ROWS), :] for s in range(TILE_ROWS)], axis=1)


def _store_token_tiles(ref, value):
    tokens = value.shape[0]
    for s in range(TILE_ROWS):
        ref[pl.ds(s, tokens, stride=TILE_ROWS), :] = value[:, s * LANES:(s + 1) * LANES]


def _proj_kernel(xb_ref, wu_ref, wqk_ref, wv_ref, c_ref, sa_ref, sb_ref, u_ref, qk_ref, v_ref):
    xb = xb_ref[...]
    u_ref[...] = _dot(xb, wu_ref[...])
    c = c_ref[...]
    sa = sa_ref[...]
    sb = sb_ref[...]
    chunk = 4 * LANES
    for j in range(2 * QK_WIDTH // chunk):
        t = _dot(xb, wqk_ref[:, j * chunk:(j + 1) * chunk])
        for s in range(chunk // LANES):
            ts = t[:, s * LANES:(s + 1) * LANES]
            up = pltpu.roll(ts, LANES - ROT_DIM // 2, axis=1)
            dn = pltpu.roll(ts, ROT_DIM // 2, axis=1)
            lo = j * chunk + s * LANES
            qk_ref[:, lo:lo + LANES] = (ts * c + up * sa + dn * sb).astype(BF16)
    for j in range(V_WIDTH // chunk):
        v_ref[:, j * chunk:(j + 1) * chunk] = _dot(
            xb, wv_ref[:, j * chunk:(j + 1) * chunk]).astype(BF16)


def _projections(xb, wu, wqk, wv, rot_c, rot_sa, rot_sb):
    T = xb.shape[0]
    tm = PROJ_ROWS
    row = lambda i: (i, 0)
    fixed = lambda i: (0, 0)
    return pl.pallas_call(
        _proj_kernel,
        grid=(T // tm,),
        in_specs=[
            pl.BlockSpec((tm, D_MODEL), row),
            pl.BlockSpec((D_MODEL, POOL_WIDTH), fixed),
            pl.BlockSpec((D_MODEL, 2 * QK_WIDTH), fixed),
            pl.BlockSpec((D_MODEL, V_WIDTH), fixed),
            pl.BlockSpec((tm, LANES), row),
            pl.BlockSpec((tm, LANES), row),
            pl.BlockSpec((tm, LANES), row),
        ],
        out_specs=[
            pl.BlockSpec((tm, POOL_WIDTH), row),
            pl.BlockSpec((tm, 2 * QK_WIDTH), row),
            pl.BlockSpec((tm, V_WIDTH), row),
        ],
        out_shape=[
            jax.ShapeDtypeStruct((T, POOL_WIDTH), F32),
            jax.ShapeDtypeStruct((T, 2 * QK_WIDTH), BF16),
            jax.ShapeDtypeStruct((T, V_WIDTH), BF16),
        ],
        compiler_params=pltpu.CompilerParams(
            dimension_semantics=("parallel",), vmem_limit_bytes=VMEM_LIMIT),
        name="projections",
    )(xb, wu, wqk, wv, rot_c, rot_sa, rot_sb)


def _pool_kernel(u_ref, w_ref, sc_ref, p_ref):
    S = u_ref.shape[0]
    row = lax.broadcasted_iota(jnp.int32, (S, POOL_GROUP_DIM), 0)
    for g, window in enumerate(POOL_WINDOWS):
        cols = slice(g * POOL_GROUP_DIM, (g + 1) * POOL_GROUP_DIM)
        u = u_ref[:, cols]
        acc = u
        span = 1
        while span < window:
            shifted = jnp.where(row >= span, pltpu.roll(acc, span, axis=0), 0.0)
            acc = acc + shifted
            span *= 2
        count = jnp.minimum(row + 1, window).astype(F32)
        d = (acc / count - u).astype(BF16)
        y = _dot(d, w_ref[g]) * sc_ref[:, cols]
        p_ref[:, cols] = y.astype(BF16)


def _pool(u, pool_w, pool_scale, B, S):
    T = u.shape[0]
    return pl.pallas_call(
        _pool_kernel,
        grid=(B,),
        in_specs=[
            pl.BlockSpec((S, POOL_WIDTH), lambda b: (b, 0)),
            pl.BlockSpec(pool_w.shape, lambda b: (0, 0, 0)),
            pl.BlockSpec((1, POOL_WIDTH), lambda b: (0, 0)),
        ],
        out_specs=pl.BlockSpec((S, POOL_WIDTH), lambda b: (b, 0)),
        out_shape=jax.ShapeDtypeStruct((T, POOL_WIDTH), BF16),
        compiler_params=pltpu.CompilerParams(
            dimension_semantics=("parallel",), vmem_limit_bytes=VMEM_LIMIT),
        name="pool",
    )(u, pool_w, pool_scale)


def _attn_kernel(lam_ref, q_ref, k_ref, v_ref, w_ref, o_ref, vt_ref, *, post_scale):
    tq = q_ref.shape[0]
    qi = pl.program_id(2)

    @pl.when(qi == 0)
    def _():
        for c in range(vt_ref.shape[0]):
            vt_ref[c] = v_ref[c * tq:(c + 1) * tq, :].astype(F32).T.astype(BF16)

    q = q_ref[...]
    lane = lax.broadcasted_iota(jnp.int32, q.shape, 1)
    zero = jnp.zeros_like(q)
    q2 = jnp.concatenate(
        [jnp.where(lane < HEAD_DIM, q, zero), jnp.where(lane >= HEAD_DIM, q, zero)], axis=0)

    def step(j, carry, diagonal):
        m, l, acc = carry
        ks = k_ref[pl.ds(pl.multiple_of(j * tq, tq), tq), :]
        s = lax.dot_general(ks, q2, (((1,), (1,)), ((), ())), preferred_element_type=F32)
        if diagonal:
            key = lax.broadcasted_iota(jnp.int32, s.shape, 0)
            qry = lax.broadcasted_iota(jnp.int32, s.shape, 1)
            qry = jnp.where(qry >= tq, qry - tq, qry)
            s = jnp.where(key <= qry, s, -jnp.inf)
        m_new = jnp.maximum(m, jnp.max(s, axis=0, keepdims=True))
        rescale = jnp.exp(m - m_new)
        p = jnp.exp(s - m_new)
        l = rescale * l + jnp.sum(p, axis=0, keepdims=True)
        acc = rescale * acc + _dot(vt_ref[j], p.astype(BF16))
        return m_new, l, acc

    init = (jnp.full((1, 2 * tq), -jnp.inf, F32), jnp.zeros((1, 2 * tq), F32),
            jnp.zeros((V_HEAD_DIM, 2 * tq), F32))
    carry = lax.fori_loop(0, qi, lambda j, c: step(j, c, False), init)
    _, l, acc = step(qi, carry, True)
    o = acc / l
    od = (o[:, :tq] - lam_ref[0] * o[:, tq:]).T
    od = od * lax.rsqrt(jnp.mean(od * od, axis=-1, keepdims=True) + LN_EPS)
    o_ref[...] = (od * w_ref[...] * post_scale).astype(BF16)


def _attention(qk, v, lam, subln_w, B, S, post_scale):
    T = qk.shape[0]
    tq = ATTN_BLOCK
    nq = S // tq
    return pl.pallas_call(
        functools.partial(_attn_kernel, post_scale=post_scale),
        grid=(B, N_HEADS, nq),
        in_specs=[
            pl.BlockSpec(memory_space=pltpu.SMEM),
            pl.BlockSpec((tq, V_HEAD_DIM), lambda b, h, i: (b * nq + i, h)),
            pl.BlockSpec((S, V_HEAD_DIM), lambda b, h, i: (b, N_HEADS + h)),
            pl.BlockSpec((S, V_HEAD_DIM), lambda b, h, i: (b, h)),
            pl.BlockSpec((1, V_HEAD_DIM), lambda b, h, i: (0, 0)),
        ],
        out_specs=pl.BlockSpec((tq, V_HEAD_DIM), lambda b, h, i: (b * nq + i, h)),
        out_shape=jax.ShapeDtypeStruct((T, V_WIDTH), BF16),
        scratch_shapes=[pltpu.VMEM((nq, V_HEAD_DIM, tq), BF16)],
        compiler_params=pltpu.CompilerParams(
            dimension_semantics=("parallel", "parallel", "arbitrary"),
            vmem_limit_bytes=VMEM_LIMIT),
        name="diff_attention",
    )(lam, qk, qk, v, subln_w)


def _mix_kernel(xb_ref, x_ref, p_ref, o_ref, wg_ref, wpb_ref, wab_ref, wout_ref, g_ref, b_ref,
                wrh_ref, wrl_ref, br_ref, x1_ref, idx_ref, gate_ref):
    xb = xb_ref[...]
    merged = jax.nn.sigmoid(_dot(xb, wg_ref[:, :D_MODEL])) * _dot(p_ref[...], wpb_ref[...])
    merged = merged + jax.nn.sigmoid(_dot(xb, wg_ref[:, D_MODEL:])) * _dot(o_ref[...], wab_ref[...])
    mix = _dot(merged.astype(BF16), wout_ref[...])
    x1 = _layer_norm(DEEPNORM_ALPHA * x_ref[...] + mix, g_ref[...], b_ref[...])
    _store_token_tiles(x1_ref, x1)

    hi = x1.astype(BF16)
    lo = (x1 - hi.astype(F32)).astype(BF16)
    logits = _dot(hi, wrh_ref[...]) + _dot(lo, wrh_ref[...]) + _dot(hi, wrl_ref[...]) + br_ref[...]

    lane = lax.broadcasted_iota(jnp.int32, logits.shape, 1)
    lane_f = lane.astype(F32)
    work = logits
    vals, idxs = [], []
    for _ in range(TOP_K):
        top = jnp.max(work, axis=1, keepdims=True)
        first = jnp.min(jnp.where(work == top, lane_f, float(LANES)), axis=1, keepdims=True)
        vals.append(top)
        idxs.append(first)
        work = jnp.where(lane_f == first, -jnp.inf, work)
    exps = [jnp.exp(v - vals[0]) for v in vals]
    denom = exps[0] + exps[1] + exps[2] + exps[3]
    idx_out = jnp.zeros(logits.shape, F32)
    gate_out = jnp.zeros(logits.shape, F32)
    for k in range(TOP_K):
        idx_out = jnp.where(lane == k, idxs[k], idx_out)
        gate_out = jnp.where(lane == k, exps[k] / denom, gate_out)
    idx_ref[...] = idx_out.astype(jnp.int32)
    gate_ref[...] = gate_out


def _mix(xb, x, p, o, wg, wpb, wab, wout, ln_g, ln_b, wr_hi, wr_lo, br):
    T = x.shape[0]
    tm = MIX_ROWS
    row = lambda i: (i, 0)
    fixed = lambda i: (0, 0)
    return pl.pallas_call(
        _mix_kernel,
        grid=(T // tm,),
        in_specs=[
            pl.BlockSpec((tm, D_MODEL), row),
            pl.BlockSpec((tm, D_MODEL), row),
            pl.BlockSpec((tm, POOL_WIDTH), row),
            pl.BlockSpec((tm, V_WIDTH), row),
            pl.BlockSpec(wg.shape, fixed),
            pl.BlockSpec(wpb.shape, fixed),
            pl.BlockSpec(wab.shape, fixed),
            pl.BlockSpec(wout.shape, fixed),
            pl.BlockSpec((1, D_MODEL), fixed),
            pl.BlockSpec((1, D_MODEL), fixed),
            pl.BlockSpec((D_MODEL, LANES), fixed),
            pl.BlockSpec((D_MODEL, LANES), fixed),
            pl.BlockSpec((1, LANES), fixed),
        ],
        out_specs=[
            pl.BlockSpec((tm * TILE_ROWS, LANES), row),
            pl.BlockSpec((tm, LANES), row),
            pl.BlockSpec((tm, LANES), row),
        ],
        out_shape=[
            jax.ShapeDtypeStruct((T * TILE_ROWS, LANES), F32),
            jax.ShapeDtypeStruct((T, LANES), jnp.int32),
            jax.ShapeDtypeStruct((T, LANES), F32),
        ],
        compiler_params=pltpu.CompilerParams(
            dimension_semantics=("parallel",), vmem_limit_bytes=VMEM_LIMIT),
        name="mixer_out_router",
    )(xb, x, p, o, wg, wpb, wab, wout, ln_g, ln_b, wr_hi, wr_lo, br)


def _tile_copy(src, dst, src_row, dst_row, sem):
    return pltpu.make_async_copy(
        src.at[pl.ds(src_row, TILE_ROWS)], dst.at[pl.ds(dst_row, TILE_ROWS)], sem)


def _gmm_kernel(be_ref, nu_ref, src_cur, src_nxt, dst_prv, x_hbm, wgu_ref, bgu_ref, wd_ref,
                bd_ref, y_hbm, xbuf, ybuf, wgu_b, wd_b, gsem, ssem):
    i = pl.program_id(0)
    n_used = nu_ref[0]
    rows = xbuf.shape[1] // TILE_ROWS
    cur = i % 2
    nxt = 1 - cur
    n_chunks = 4
    cw = D_FF // n_chunks
    per_chunk = rows // n_chunks

    def wait_gather(slot, count):
        n = count * TILE_ROWS
        pltpu.make_async_copy(x_hbm.at[pl.ds(0, n)], xbuf.at[slot, pl.ds(0, n)],
                              gsem.at[slot]).wait()

    def wait_scatter(count):
        n = count * TILE_ROWS
        pltpu.make_async_copy(ybuf.at[nxt, pl.ds(0, n)], y_hbm.at[pl.ds(0, n)], ssem).wait()

    def gather_row(table, slot, r):
        src_row = pl.multiple_of(table[0, 0, r], TILE_ROWS)
        dst_row = r * TILE_ROWS if isinstance(r, int) else pl.multiple_of(r * TILE_ROWS, TILE_ROWS)
        _tile_copy(x_hbm, xbuf.at[slot], src_row, dst_row, gsem.at[slot]).start()

    def scatter_row(r):
        src_row = r * TILE_ROWS if isinstance(r, int) else pl.multiple_of(r * TILE_ROWS, TILE_ROWS)
        dst_row = pl.multiple_of(dst_prv[0, 0, r], TILE_ROWS)
        _tile_copy(ybuf.at[nxt], y_hbm, src_row, dst_row, ssem).start()

    @pl.when(i == 0)
    def _():
        def issue(r, c):
            gather_row(src_cur, 0, r)
            return c

        lax.fori_loop(0, rows, issue, 0)
        wait_gather(0, rows - per_chunk)
        ybuf[1] = jnp.zeros(ybuf.shape[1:], F32)

    @pl.when(i <= n_used)
    def _():
        wait_gather(cur, per_chunk)

    @pl.when((i < n_used) & ((i == 0) | (be_ref[i] != be_ref[jnp.maximum(i - 1, 0)])))
    def _():
        step = 128

        def cast(c, carry):
            sl = pl.ds(pl.multiple_of(c * step, step), step)
            wgu_b[sl, :] = wgu_ref[sl, :].astype(BF16)
            wd_b[sl, :] = wd_ref[sl, :].astype(BF16)
            return carry

        lax.fori_loop(0, D_MODEL // step, cast, 0)

    @pl.when(i < n_used)
    def _():
        xb = _load_token_tiles(xbuf.at[cur], rows).astype(BF16)
        y = jnp.zeros((rows, D_MODEL), F32) + bd_ref[...]
        for c in range(n_chunks):
            for r in range(c * per_chunk, (c + 1) * per_chunk):
                gather_row(src_nxt, nxt, r)
                scatter_row(r)
            g_cols = slice(c * cw, (c + 1) * cw)
            u_cols = slice(D_FF + c * cw, D_FF + (c + 1) * cw)
            hg = _dot(xb, wgu_b[:, g_cols]) + bgu_ref[:, g_cols]
            hu = _dot(xb, wgu_b[:, u_cols]) + bgu_ref[:, u_cols]
            hg = jnp.minimum(hg, SWIGLU_LIMIT)
            hu = jnp.clip(hu, -SWIGLU_LIMIT, SWIGLU_LIMIT)
            act = (hu + 1.0) * hg * jax.nn.sigmoid(SWIGLU_ALPHA * hg)
            y = y + _dot(act.astype(BF16), wd_b[g_cols, :])
            if c >= 1:
                wait_gather(nxt, per_chunk)
                wait_scatter(per_chunk)
        _store_token_tiles(ybuf.at[cur], y)
        wait_scatter(per_chunk)

    @pl.when(i == n_used)
    def _():
        def issue(r, c):
            scatter_row(r)
            return c

        lax.fori_loop(0, rows, issue, 0)
        wait_scatter(rows)


def _expert_blocks(layer, block_e, n_used, src, dst, x1, w_gu, b_gu, w_down, b_down):
    T = x1.shape[0] // TILE_ROWS
    rows = EXPERT_ROWS
    steps = block_e.shape[0]
    depth = w_gu.shape[0]
    grid_spec = pltpu.PrefetchScalarGridSpec(
        num_scalar_prefetch=2,
        grid=(steps,),
        in_specs=[
            pl.BlockSpec((1, 1, rows), lambda i, be, nu: (i, 0, 0), memory_space=pltpu.SMEM),
            pl.BlockSpec((1, 1, rows), lambda i, be, nu: (jnp.minimum(i + 1, steps - 1), 0, 0),
                         memory_space=pltpu.SMEM),
            pl.BlockSpec((1, 1, rows), lambda i, be, nu: (i, 0, 0), memory_space=pltpu.SMEM),
            pl.BlockSpec(memory_space=pl.ANY),
            pl.BlockSpec((None, None, D_MODEL, 2 * D_FF), lambda i, be, nu: (layer, be[i], 0, 0)),
            pl.BlockSpec((None, None, 1, 2 * D_FF), lambda i, be, nu: (layer, be[i], 0, 0)),
            pl.BlockSpec((None, None, D_FF, D_MODEL), lambda i, be, nu: (layer, be[i], 0, 0)),
            pl.BlockSpec((None, None, 1, D_MODEL), lambda i, be, nu: (layer, be[i], 0, 0)),
        ],
        out_specs=pl.BlockSpec(memory_space=pl.ANY),
        scratch_shapes=[
            pltpu.VMEM((2, rows * TILE_ROWS, LANES), F32),
            pltpu.VMEM((2, rows * TILE_ROWS, LANES), F32),
            pltpu.VMEM((D_MODEL, 2 * D_FF), BF16),
            pltpu.VMEM((D_FF, D_MODEL), BF16),
            pltpu.SemaphoreType.DMA((2,)),
            pltpu.SemaphoreType.DMA,
        ],
    )
    return pl.pallas_call(
        _gmm_kernel,
        grid_spec=grid_spec,
        out_shape=jax.ShapeDtypeStruct(((TOP_K * T + rows) * TILE_ROWS, LANES), F32),
        compiler_params=pltpu.CompilerParams(
            dimension_semantics=("arbitrary",), vmem_limit_bytes=VMEM_LIMIT),
        name="expert_blocks",
    )(block_e, n_used, src.reshape(steps, 1, rows), src.reshape(steps, 1, rows),
      dst.reshape(steps, 1, rows), x1, w_gu,
      b_gu.reshape(depth, N_EXPERTS, 1, 2 * D_FF), w_down,
      b_down.reshape(depth, N_EXPERTS, 1, D_MODEL))


def _combine_kernel(y0_ref, y1_ref, y2_ref, y3_ref, gate_ref, x1_ref, g_ref, b_ref, x2_ref, xb2_ref):
    tm = x2_ref.shape[0]
    gate = gate_ref[...]
    z = DEEPNORM_ALPHA * _load_token_tiles(x1_ref, tm)
    for k, y_ref in enumerate((y0_ref, y1_ref, y2_ref, y3_ref)):
        z = z + gate[:, k:k + 1] * _load_token_tiles(y_ref, tm)
    x2 = _layer_norm(z, g_ref[...], b_ref[...])
    x2_ref[...] = x2
    xb2_ref[...] = x2.astype(BF16)


def _combine(ys, gate, x1, ln_g, ln_b):
    T = x1.shape[0] // TILE_ROWS
    tm = COMBINE_ROWS
    nt = T // tm
    row = lambda i: (i, 0)
    fixed = lambda i: (0, 0)
    y_specs = [pl.BlockSpec((tm * TILE_ROWS, LANES),
                            functools.partial(lambda k, i: (k * nt + i, 0), k))
               for k in range(TOP_K)]
    return pl.pallas_call(
        _combine_kernel,
        grid=(nt,),
        in_specs=y_specs + [
            pl.BlockSpec((tm, LANES), row),
            pl.BlockSpec((tm * TILE_ROWS, LANES), row),
            pl.BlockSpec((1, D_MODEL), fixed),
            pl.BlockSpec((1, D_MODEL), fixed),
        ],
        out_specs=[
            pl.BlockSpec((tm, D_MODEL), row),
            pl.BlockSpec((tm, D_MODEL), row),
        ],
        out_shape=[
            jax.ShapeDtypeStruct((T, D_MODEL), F32),
            jax.ShapeDtypeStruct((T, D_MODEL), BF16),
        ],
        compiler_params=pltpu.CompilerParams(
            dimension_semantics=("parallel",), vmem_limit_bytes=VMEM_LIMIT),
        name="combine_ln",
    )(ys, ys, ys, ys, gate, x1, ln_g, ln_b)


def _routing_tables(top_idx):
    T = top_idx.shape[0]
    A = T * TOP_K
    rows_per = EXPERT_ROWS
    flat_e = top_idx.reshape(A)
    key = jnp.sort(flat_e * A + jnp.arange(A, dtype=jnp.int32))
    order = key % A
    experts = jnp.arange(N_EXPERTS, dtype=jnp.int32)
    counts = jnp.sum((flat_e[:, None] == experts[None, :]).astype(jnp.int32), axis=0)
    padded = (counts + rows_per - 1) // rows_per * rows_per
    start = jnp.cumsum(counts) - counts
    pend = jnp.cumsum(padded)
    pstart = pend - padded
    steps = A // rows_per + N_EXPERTS + 1
    first_row = jnp.arange(steps, dtype=jnp.int32) * rows_per
    block_e = jnp.minimum(
        jnp.sum((pend[None, :] <= first_row[:, None]).astype(jnp.int32), axis=1), N_EXPERTS - 1)
    r = jnp.arange(steps * rows_per, dtype=jnp.int32)
    e_r = jnp.repeat(block_e, rows_per)
    off = r - pstart[e_r]
    valid = off < counts[e_r]
    a = order[jnp.clip(start[e_r] + off, 0, A - 1)]
    tok = a // TOP_K
    src = jnp.where(valid, tok, 0).astype(jnp.int32) * TILE_ROWS
    spare = TOP_K * T + r % rows_per
    dst = jnp.where(valid, (a % TOP_K) * T + tok, spare).astype(jnp.int32)
    dst = jnp.concatenate([spare[:rows_per], dst[:-rows_per]]).astype(jnp.int32) * TILE_ROWS
    n_used = (pend[-1] // rows_per).astype(jnp.int32).reshape(1)
    return src, dst, block_e.astype(jnp.int32), n_used


def _rotary_tables(positions):
    half = ROT_DIM // 2
    inv_freq = ROPE_THETA ** (-jnp.arange(0, ROT_DIM, 2, dtype=F32) / ROT_DIM)
    ang = positions.reshape(-1).astype(F32)[:, None] * inv_freq
    cos, sin = jnp.cos(ang), jnp.sin(ang)
    T = ang.shape[0]
    ones = jnp.ones((T, HEAD_DIM - ROT_DIM), F32)
    zeros = jnp.zeros((T, HEAD_DIM - ROT_DIM), F32)
    zh = jnp.zeros((T, half), F32)
    c = jnp.concatenate([cos, cos, ones], axis=1)
    sa = jnp.concatenate([-sin, zh, zeros], axis=1)
    sb = jnp.concatenate([zh, sin, zeros], axis=1)
    reps = LANES // HEAD_DIM
    return jnp.tile(c, (1, reps)), jnp.tile(sa, (1, reps)), jnp.tile(sb, (1, reps))


def kernel(x, positions, w_in, pool_w, pool_scale, w_pool_branch, w_attn_branch, lambda_q1, lambda_k1, lambda_q2, lambda_k2, subln_w, w_out, ln1_g, ln1_b, w_router, b_router, w_gu, b_gu, w_down, b_down, ln2_g, ln2_b):
    B, S, D = x.shape
    assert D == D_MODEL and S % ATTN_BLOCK == 0 and w_in.shape[0] == DEPTH
    T = B * S
    assert T % PROJ_ROWS == 0 and T % MIX_ROWS == 0 and T % COMBINE_ROWS == 0
    assert (T * TOP_K) % EXPERT_ROWS == 0
    rot_c, rot_sa, rot_sb = _rotary_tables(positions)
    xf = x.reshape(T, D)
    xb = xf.astype(BF16)
    o_qk = POOL_WIDTH
    o_v = o_qk + 2 * QK_WIDTH
    o_g = o_v + V_WIDTH
    for l in range(DEPTH):
        wl = w_in[l]
        wu = wl[:, :o_qk].astype(BF16)
        wq = wl[:, o_qk:o_qk + QK_WIDTH] * (HEAD_DIM ** -0.5)
        wqk = jnp.concatenate([wq, wl[:, o_qk + QK_WIDTH:o_v]], axis=1).astype(BF16)
        wv = wl[:, o_v:o_g].astype(BF16)
        wg = wl[:, o_g:].astype(BF16)
        u, qk, v = _projections(xb, wu, wqk, wv, rot_c, rot_sa, rot_sb)
        p = _pool(u, pool_w[l].astype(BF16), pool_scale[l].reshape(1, POOL_WIDTH), B, S)
        lambda_init = 0.8 - 0.6 * math.exp(-0.3 * l)
        lam = (jnp.exp(jnp.sum(lambda_q1[l] * lambda_k1[l]))
               - jnp.exp(jnp.sum(lambda_q2[l] * lambda_k2[l])) + lambda_init).reshape(1)
        o = _attention(qk, v, lam, subln_w[l].reshape(1, V_HEAD_DIM), B, S, 1.0 - lambda_init)
        wr = jnp.zeros((D, LANES), F32).at[:, :N_EXPERTS].set(w_router[l])
        wr_hi = wr.astype(BF16)
        wr_lo = (wr - wr_hi.astype(F32)).astype(BF16)
        br = jnp.full((1, LANES), NEG_BIG, F32).at[0, :N_EXPERTS].set(b_router[l])
        x1, idx, gate = _mix(
            xb, xf, p, o, wg, w_pool_branch[l].astype(BF16), w_attn_branch[l].astype(BF16),
            w_out[l].astype(BF16), ln1_g[l].reshape(1, D), ln1_b[l].reshape(1, D),
            wr_hi, wr_lo, br)
        src, dst, block_e, n_used = _routing_tables(idx[:, :TOP_K])
        ys = _expert_blocks(l, block_e, n_used, src, dst, x1, w_gu, b_gu, w_down, b_down)
        xf, xb = _combine(ys, gate, x1, ln2_g[l].reshape(1, D), ln2_b[l].reshape(1, D))
    return xf.reshape(B, S, D)
```

```python
import functools
import math

import jax
import jax.numpy as jnp
from jax import lax
from jax.experimental import pallas as pl
from jax.experimental.pallas import tpu as pltpu

F32 = jnp.float32
BF16 = jnp.bfloat16

D_MODEL = 1024
DEPTH = 2
POOL_WINDOWS = (2, 4, 8, 16)
POOL_GROUP_DIM = 128
POOL_WIDTH = POOL_GROUP_DIM * len(POOL_WINDOWS)
N_HEADS = 8
HEAD_DIM = 64
V_HEAD_DIM = 2 * HEAD_DIM
QK_WIDTH = N_HEADS * 2 * HEAD_DIM
V_WIDTH = N_HEADS * V_HEAD_DIM
ROPE_THETA = 500000.0
ROT_DIM = HEAD_DIM // 4
N_EXPERTS = 32
TOP_K = 4
D_FF = D_MODEL
SWIGLU_ALPHA = 1.702
SWIGLU_LIMIT = 7.0
LN_EPS = 1e-5
DEEPNORM_ALPHA = (2 * DEPTH) ** 0.25

LANES = 128
VMEM_LIMIT = 48 * 1024 * 1024

PROJ_ROWS = 512
ATTN_BLOCK = 512
MIX_ROWS = 256
EXPERT_ROWS = 256
COMBINE_ROWS = 256
NEG_BIG = -1e30


def _dot(a, b):
    return jnp.dot(a, b, preferred_element_type=F32)


def _layer_norm(z, g, b):
    mu = jnp.mean(z, axis=-1, keepdims=True)
    zc = z - mu
    var = jnp.mean(zc * zc, axis=-1, keepdims=True)
    return zc * lax.rsqrt(var + LN_EPS) * g + b


TILE_ROWS = D_MODEL // LANES


def _load_token_tiles(ref, tokens):
    return jnp.concatenate(
        [ref[pl.ds(s, tokens, stride=TILE_ROWS), :] for s in range(TILE_ROWS)], axis=1)


def _store_token_tiles(ref, value):
    tokens = value.shape[0]
    for s in range(TILE_ROWS):
        ref[pl.ds(s, tokens, stride=TILE_ROWS), :] = value[:, s * LANES:(s + 1) * LANES]


def _proj_kernel(xb_ref, wu_ref, wqk_ref, wv_ref, c_ref, sa_ref, sb_ref, u_ref, qk_ref, v_ref):
    xb = xb_ref[...]
    u_ref[...] = _dot(xb, wu_ref[...])
    c = c_ref[...]
    sa = sa_ref[...]
    sb = sb_ref[...]
    chunk = 4 * LANES
    for j in range(2 * QK_WIDTH // chunk):
        t = _dot(xb, wqk_ref[:, j * chunk:(j + 1) * chunk])
        for s in range(chunk // LANES):
            ts = t[:, s * LANES:(s + 1) * LANES]
            up = pltpu.roll(ts, LANES - ROT_DIM // 2, axis=1)
            dn = pltpu.roll(ts, ROT_DIM // 2, axis=1)
            lo = j * chunk + s * LANES
            qk_ref[:, lo:lo + LANES] = (ts * c + up * sa + dn * sb).astype(BF16)
    for j in range(V_WIDTH // chunk):
        v_ref[:, j * chunk:(j + 1) * chunk] = _dot(
            xb, wv_ref[:, j * chunk:(j + 1) * chunk]).astype(BF16)


def _projections(xb, wu, wqk, wv, rot_c, rot_sa, rot_sb):
    T = xb.shape[0]
    tm = PROJ_ROWS
    row = lambda i: (i, 0)
    fixed = lambda i: (0, 0)
    return pl.pallas_call(
        _proj_kernel,
        grid=(T // tm,),
        in_specs=[
            pl.BlockSpec((tm, D_MODEL), row),
            pl.BlockSpec((D_MODEL, POOL_WIDTH), fixed),
            pl.BlockSpec((D_MODEL, 2 * QK_WIDTH), fixed),
            pl.BlockSpec((D_MODEL, V_WIDTH), fixed),
            pl.BlockSpec((tm, LANES), row),
            pl.BlockSpec((tm, LANES), row),
            pl.BlockSpec((tm, LANES), row),
        ],
        out_specs=[
            pl.BlockSpec((tm, POOL_WIDTH), row),
            pl.BlockSpec((tm, 2 * QK_WIDTH), row),
            pl.BlockSpec((tm, V_WIDTH), row),
        ],
        out_shape=[
            jax.ShapeDtypeStruct((T, POOL_WIDTH), F32),
            jax.ShapeDtypeStruct((T, 2 * QK_WIDTH), BF16),
            jax.ShapeDtypeStruct((T, V_WIDTH), BF16),
        ],
        compiler_params=pltpu.CompilerParams(
            dimension_semantics=("parallel",), vmem_limit_bytes=VMEM_LIMIT),
        name="projections",
    )(xb, wu, wqk, wv, rot_c, rot_sa, rot_sb)


def _pool_kernel(u_ref, w_ref, sc_ref, p_ref):
    S = u_ref.shape[0]
    row = lax.broadcasted_iota(jnp.int32, (S, POOL_GROUP_DIM), 0)
    for g, window in enumerate(POOL_WINDOWS):
        cols = slice(g * POOL_GROUP_DIM, (g + 1) * POOL_GROUP_DIM)
        u = u_ref[:, cols]
        acc = u
        span = 1
        while span < window:
            shifted = jnp.where(row >= span, pltpu.roll(acc, span, axis=0), 0.0)
            acc = acc + shifted
            span *= 2
        count = jnp.minimum(row + 1, window).astype(F32)
        d = (acc / count - u).astype(BF16)
        y = _dot(d, w_ref[g]) * sc_ref[:, cols]
        p_ref[:, cols] = y.astype(BF16)


def _pool(u, pool_w, pool_scale, B, S):
    T = u.shape[0]
    return pl.pallas_call(
        _pool_kernel,
        grid=(B,),
        in_specs=[
            pl.BlockSpec((S, POOL_WIDTH), lambda b: (b, 0)),
            pl.BlockSpec(pool_w.shape, lambda b: (0, 0, 0)),
            pl.BlockSpec((1, POOL_WIDTH), lambda b: (0, 0)),
        ],
        out_specs=pl.BlockSpec((S, POOL_WIDTH), lambda b: (b, 0)),
        out_shape=jax.ShapeDtypeStruct((T, POOL_WIDTH), BF16),
        compiler_params=pltpu.CompilerParams(
            dimension_semantics=("parallel",), vmem_limit_bytes=VMEM_LIMIT),
        name="pool",
    )(u, pool_w, pool_scale)


def _attn_kernel(lam_ref, q_ref, k_ref, v_ref, w_ref, o_ref, vt_ref, *, post_scale):
    tq = q_ref.shape[0]
    qi = pl.program_id(2)

    @pl.when(qi == 0)
    def _():
        for c in range(vt_ref.shape[0]):
            vt_ref[c] = v_ref[c * tq:(c + 1) * tq, :].astype(F32).T.astype(BF16)

    q = q_ref[...]
    lane = lax.broadcasted_iota(jnp.int32, q.shape, 1)
    zero = jnp.zeros_like(q)
    q2 = jnp.concatenate(
        [jnp.where(lane < HEAD_DIM, q, zero), jnp.where(lane >= HEAD_DIM, q, zero)], axis=0)

    def step(j, carry, diagonal):
        m, l, acc = carry
        ks = k_ref[pl.ds(pl.multiple_of(j * tq, tq), tq), :]
        s = lax.dot_general(ks, q2, (((1,), (1,)), ((), ())), preferred_element_type=F32)
        if diagonal:
            key = lax.broadcasted_iota(jnp.int32, s.shape, 0)
            qry = lax.broadcasted_iota(jnp.int32, s.shape, 1)
            qry = jnp.where(qry >= tq, qry - tq, qry)
            s = jnp.where(key <= qry, s, -jnp.inf)
        m_new = jnp.maximum(m, jnp.max(s, axis=0, keepdims=True))
        rescale = jnp.exp(m - m_new)
        p = jnp.exp(s - m_new)
        l = rescale * l + jnp.sum(p, axis=0, keepdims=True)
        acc = rescale * acc + _dot(vt_ref[j], p.astype(BF16))
        return m_new, l, acc

    init = (jnp.full((1, 2 * tq), -jnp.inf, F32), jnp.zeros((1, 2 * tq), F32),
            jnp.zeros((V_HEAD_DIM, 2 * tq), F32))
    carry = lax.fori_loop(0, qi, lambda j, c: step(j, c, False), init)
    _, l, acc = step(qi, carry, True)
    o = acc / l
    od = (o[:, :tq] - lam_ref[0] * o[:, tq:]).T
    od = od * lax.rsqrt(jnp.mean(od * od, axis=-1, keepdims=True) + LN_EPS)
    o_ref[...] = (od * w_ref[...] * post_scale).astype(BF16)


def _attention(qk, v, lam, subln_w, B, S, post_scale):
    T = qk.shape[0]
    tq = ATTN_BLOCK
    nq = S // tq
    return pl.pallas_call(
        functools.partial(_attn_kernel, post_scale=post_scale),
        grid=(B, N_HEADS, nq),
        in_specs=[
            pl.BlockSpec(memory_space=pltpu.SMEM),
            pl.BlockSpec((tq, V_HEAD_DIM), lambda b, h, i: (b * nq + i, h)),
            pl.BlockSpec((S, V_HEAD_DIM), lambda b, h, i: (b, N_HEADS + h)),
            pl.BlockSpec((S, V_HEAD_DIM), lambda b, h, i: (b, h)),
            pl.BlockSpec((1, V_HEAD_DIM), lambda b, h, i: (0, 0)),
        ],
        out_specs=pl.BlockSpec((tq, V_HEAD_DIM), lambda b, h, i: (b * nq + i, h)),
        out_shape=jax.ShapeDtypeStruct((T, V_WIDTH), BF16),
        scratch_shapes=[pltpu.VMEM((nq, V_HEAD_DIM, tq), BF16)],
        compiler_params=pltpu.CompilerParams(
            dimension_semantics=("parallel", "parallel", "arbitrary"),
            vmem_limit_bytes=VMEM_LIMIT),
        name="diff_attention",
    )(lam, qk, qk, v, subln_w)


def _mix_kernel(xb_ref, x_ref, p_ref, o_ref, wg_ref, wpb_ref, wab_ref, wout_ref, g_ref, b_ref,
                wrh_ref, wrl_ref, br_ref, x1_ref, idx_ref, gate_ref):
    xb = xb_ref[...]
    merged = jax.nn.sigmoid(_dot(xb, wg_ref[:, :D_MODEL])) * _dot(p_ref[...], wpb_ref[...])
    merged = merged + jax.nn.sigmoid(_dot(xb, wg_ref[:, D_MODEL:])) * _dot(o_ref[...], wab_ref[...])
    mix = _dot(merged.astype(BF16), wout_ref[...])
    x1 = _layer_norm(DEEPNORM_ALPHA * x_ref[...] + mix, g_ref[...], b_ref[...])
    _store_token_tiles(x1_ref, x1)

    hi = x1.astype(BF16)
    lo = (x1 - hi.astype(F32)).astype(BF16)
    logits = _dot(hi, wrh_ref[...]) + _dot(lo, wrh_ref[...]) + _dot(hi, wrl_ref[...]) + br_ref[...]

    lane = lax.broadcasted_iota(jnp.int32, logits.shape, 1)
    lane_f = lane.astype(F32)
    work = logits
    vals, idxs = [], []
    for _ in range(TOP_K):
        top = jnp.max(work, axis=1, keepdims=True)
        first = jnp.min(jnp.where(work == top, lane_f, float(LANES)), axis=1, keepdims=True)
        vals.append(top)
        idxs.append(first)
        work = jnp.where(lane_f == first, -jnp.inf, work)
    exps = [jnp.exp(v - vals[0]) for v in vals]
    denom = exps[0] + exps[1] + exps[2] + exps[3]
    idx_out = jnp.zeros(logits.shape, F32)
    gate_out = jnp.zeros(logits.shape, F32)
    for k in range(TOP_K):
        idx_out = jnp.where(lane == k, idxs[k], idx_out)
        gate_out = jnp.where(lane == k, exps[k] / denom, gate_out)
    idx_ref[...] = idx_out.astype(jnp.int32)
    gate_ref[...] = gate_out


def _mix(xb, x, p, o, wg, wpb, wab, wout, ln_g, ln_b, wr_hi, wr_lo, br):
    T = x.shape[0]
    tm = MIX_ROWS
    row = lambda i: (i, 0)
    fixed = lambda i: (0, 0)
    return pl.pallas_call(
        _mix_kernel,
        grid=(T // tm,),
        in_specs=[
            pl.BlockSpec((tm, D_MODEL), row),
            pl.BlockSpec((tm, D_MODEL), row),
            pl.BlockSpec((tm, POOL_WIDTH), row),
            pl.BlockSpec((tm, V_WIDTH), row),
            pl.BlockSpec(wg.shape, fixed),
            pl.BlockSpec(wpb.shape, fixed),
            pl.BlockSpec(wab.shape, fixed),
            pl.BlockSpec(wout.shape, fixed),
            pl.BlockSpec((1, D_MODEL), fixed),
            pl.BlockSpec((1, D_MODEL), fixed),
            pl.BlockSpec((D_MODEL, LANES), fixed),
            pl.BlockSpec((D_MODEL, LANES), fixed),
            pl.BlockSpec((1, LANES), fixed),
        ],
        out_specs=[
            pl.BlockSpec((tm * TILE_ROWS, LANES), row),
            pl.BlockSpec((tm, LANES), row),
            pl.BlockSpec((tm, LANES), row),
        ],
        out_shape=[
            jax.ShapeDtypeStruct((T * TILE_ROWS, LANES), F32),
            jax.ShapeDtypeStruct((T, LANES), jnp.int32),
            jax.ShapeDtypeStruct((T, LANES), F32),
        ],
        compiler_params=pltpu.CompilerParams(
            dimension_semantics=("parallel",), vmem_limit_bytes=VMEM_LIMIT),
        name="mixer_out_router",
    )(xb, x, p, o, wg, wpb, wab, wout, ln_g, ln_b, wr_hi, wr_lo, br)


def _tile_copy(src, dst, src_row, dst_row, sem):
    return pltpu.make_async_copy(
        src.at[pl.ds(src_row, TILE_ROWS)], dst.at[pl.ds(dst_row, TILE_ROWS)], sem)


def _gmm_kernel(be_ref, nu_ref, src_b0, src_b1, src_n2, dst_prv, x_hbm, wgu_ref, bgu_ref, wd_ref,
                bd_ref, y_hbm, xbuf, ybuf, wgu_b, wd_b, gsem, ssem):
    i = pl.program_id(0)
    n_used = nu_ref[0]
    rows = xbuf.shape[1] // TILE_ROWS
    x_cur, x_n1, x_n2 = i % 3, (i + 1) % 3, (i + 2) % 3
    y_cur = i % 2
    y_prv = 1 - y_cur
    n_chunks = 4
    cw = D_FF // n_chunks
    per_chunk = rows // n_chunks
    spare_b = y_hbm.shape[0] - rows * TILE_ROWS

    def wait_gather(slot, count):
        n = count * TILE_ROWS
        pltpu.make_async_copy(x_hbm.at[pl.ds(0, n)], xbuf.at[slot, pl.ds(0, n)],
                              gsem.at[slot]).wait()

    def wait_scatter(slot, count):
        n = count * TILE_ROWS
        pltpu.make_async_copy(ybuf.at[slot, pl.ds(0, n)], y_hbm.at[pl.ds(0, n)],
                              ssem.at[slot]).wait()

    def tile_row(r):
        return r * TILE_ROWS if isinstance(r, int) else pl.multiple_of(r * TILE_ROWS, TILE_ROWS)

    def gather_row(table, slot, r):
        src_row = pl.multiple_of(table[0, 0, r], TILE_ROWS)
        _tile_copy(x_hbm, xbuf.at[slot], src_row, tile_row(r), gsem.at[slot]).start(priority=0)

    def scatter_row(slot, r, dst_row):
        _tile_copy(ybuf.at[slot], y_hbm, tile_row(r), dst_row, ssem.at[slot]).start(priority=1)

    @pl.when(i == 0)
    def _():
        ybuf[...] = jnp.zeros(ybuf.shape, F32)

        def issue(r, c):
            gather_row(src_b0, 0, r)
            gather_row(src_b1, 1, r)
            scatter_row(0, r, spare_b + tile_row(r))
            return c

        lax.fori_loop(0, rows, issue, 0)
        wait_gather(0, rows)

    @pl.when((i < n_used) & ((i == 0) | (be_ref[i] != be_ref[jnp.maximum(i - 1, 0)])))
    def _():
        step = 128

        def cast(c, carry):
            sl = pl.ds(pl.multiple_of(c * step, step), step)
            wgu_b[sl, :] = wgu_ref[sl, :].astype(BF16)
            wd_b[sl, :] = wd_ref[sl, :].astype(BF16)
            return carry

        lax.fori_loop(0, D_MODEL // step, cast, 0)

    @pl.when(i < n_used)
    def _():
        xb = _load_token_tiles(xbuf.at[x_cur], rows).astype(BF16)
        y = jnp.zeros((rows, D_MODEL), F32) + bd_ref[...]
        for c in range(n_chunks):
            for r in range(c * per_chunk, (c + 1) * per_chunk):
                gather_row(src_n2, x_n2, r)
                scatter_row(y_prv, r, pl.multiple_of(dst_prv[0, 0, r], TILE_ROWS))
            g_cols = slice(c * cw, (c + 1) * cw)
            u_cols = slice(D_FF + c * cw, D_FF + (c + 1) * cw)
            hg = _dot(xb, wgu_b[:, g_cols]) + bgu_ref[:, g_cols]
            hu = _dot(xb, wgu_b[:, u_cols]) + bgu_ref[:, u_cols]
            hg = jnp.minimum(hg, SWIGLU_LIMIT)
            hu = jnp.clip(hu, -SWIGLU_LIMIT, SWIGLU_LIMIT)
            act = (hu + 1.0) * hg * jax.nn.sigmoid(SWIGLU_ALPHA * hg)
            y = y + _dot(act.astype(BF16), wd_b[g_cols, :])
            wait_gather(x_n1, per_chunk)
            wait_scatter(y_cur, per_chunk)
        _store_token_tiles(ybuf.at[y_cur], y)

    @pl.when(i == n_used)
    def _():
        def issue(r, c):
            scatter_row(y_prv, r, pl.multiple_of(dst_prv[0, 0, r], TILE_ROWS))
            return c

        lax.fori_loop(0, rows, issue, 0)
        wait_gather(x_n1, rows)
        wait_scatter(y_cur, rows)
        wait_scatter(y_prv, rows)


def _expert_blocks(layer, block_e, n_used, src, dst, x1, w_gu, b_gu, w_down, b_down):
    T = x1.shape[0] // TILE_ROWS
    rows = EXPERT_ROWS
    steps = block_e.shape[0]
    depth = w_gu.shape[0]
    src = src.reshape(steps + 1, 1, rows)
    grid_spec = pltpu.PrefetchScalarGridSpec(
        num_scalar_prefetch=2,
        grid=(steps,),
        in_specs=[
            pl.BlockSpec((1, 1, rows), lambda i, be, nu: (0, 0, 0), memory_space=pltpu.SMEM),
            pl.BlockSpec((1, 1, rows), lambda i, be, nu: (1, 0, 0), memory_space=pltpu.SMEM),
            pl.BlockSpec((1, 1, rows), lambda i, be, nu: (jnp.minimum(i + 2, steps), 0, 0),
                         memory_space=pltpu.SMEM),
            pl.BlockSpec((1, 1, rows), lambda i, be, nu: (i, 0, 0), memory_space=pltpu.SMEM),
            pl.BlockSpec(memory_space=pl.ANY),
            pl.BlockSpec((None, None, D_MODEL, 2 * D_FF), lambda i, be, nu: (layer, be[i], 0, 0)),
            pl.BlockSpec((None, None, 1, 2 * D_FF), lambda i, be, nu: (layer, be[i], 0, 0)),
            pl.BlockSpec((None, None, D_FF, D_MODEL), lambda i, be, nu: (layer, be[i], 0, 0)),
            pl.BlockSpec((None, None, 1, D_MODEL), lambda i, be, nu: (layer, be[i], 0, 0)),
        ],
        out_specs=pl.BlockSpec(memory_space=pl.ANY),
        scratch_shapes=[
            pltpu.VMEM((3, rows * TILE_ROWS, LANES), F32),
            pltpu.VMEM((2, rows * TILE_ROWS, LANES), F32),
            pltpu.VMEM((D_MODEL, 2 * D_FF), BF16),
            pltpu.VMEM((D_FF, D_MODEL), BF16),
            pltpu.SemaphoreType.DMA((3,)),
            pltpu.SemaphoreType.DMA((2,)),
        ],
    )
    return pl.pallas_call(
        _gmm_kernel,
        grid_spec=grid_spec,
        out_shape=jax.ShapeDtypeStruct(((TOP_K * T + 2 * rows) * TILE_ROWS, LANES), F32),
        compiler_params=pltpu.CompilerParams(
            dimension_semantics=("arbitrary",), vmem_limit_bytes=VMEM_LIMIT),
        name="expert_blocks",
    )(block_e, n_used, src, src, src, dst.reshape(steps, 1, rows), x1, w_gu,
      b_gu.reshape(depth, N_EXPERTS, 1, 2 * D_FF), w_down,
      b_down.reshape(depth, N_EXPERTS, 1, D_MODEL))


def _combine_kernel(y0_ref, y1_ref, y2_ref, y3_ref, gate_ref, x1_ref, g_ref, b_ref, x2_ref, xb2_ref):
    tm = x2_ref.shape[0]
    gate = gate_ref[...]
    z = DEEPNORM_ALPHA * _load_token_tiles(x1_ref, tm)
    for k, y_ref in enumerate((y0_ref, y1_ref, y2_ref, y3_ref)):
        z = z + gate[:, k:k + 1] * _load_token_tiles(y_ref, tm)
    x2 = _layer_norm(z, g_ref[...], b_ref[...])
    x2_ref[...] = x2
    xb2_ref[...] = x2.astype(BF16)


def _combine(ys, gate, x1, ln_g, ln_b):
    T = x1.shape[0] // TILE_ROWS
    tm = COMBINE_ROWS
    nt = T // tm
    row = lambda i: (i, 0)
    fixed = lambda i: (0, 0)
    y_specs = [pl.BlockSpec((tm * TILE_ROWS, LANES),
                            functools.partial(lambda k, i: (k * nt + i, 0), k))
               for k in range(TOP_K)]
    return pl.pallas_call(
        _combine_kernel,
        grid=(nt,),
        in_specs=y_specs + [
            pl.BlockSpec((tm, LANES), row),
            pl.BlockSpec((tm * TILE_ROWS, LANES), row),
            pl.BlockSpec((1, D_MODEL), fixed),
            pl.BlockSpec((1, D_MODEL), fixed),
        ],
        out_specs=[
            pl.BlockSpec((tm, D_MODEL), row),
            pl.BlockSpec((tm, D_MODEL), row),
        ],
        out_shape=[
            jax.ShapeDtypeStruct((T, D_MODEL), F32),
            jax.ShapeDtypeStruct((T, D_MODEL), BF16),
        ],
        compiler_params=pltpu.CompilerParams(
            dimension_semantics=("parallel",), vmem_limit_bytes=VMEM_LIMIT),
        name="combine_ln",
    )(ys, ys, ys, ys, gate, x1, ln_g, ln_b)


def _routing_tables(top_idx):
    T = top_idx.shape[0]
    A = T * TOP_K
    rows_per = EXPERT_ROWS
    flat_e = top_idx.reshape(A)
    key = jnp.sort(flat_e * A + jnp.arange(A, dtype=jnp.int32))
    order = key % A
    experts = jnp.arange(N_EXPERTS, dtype=jnp.int32)
    counts = jnp.sum((flat_e[:, None] == experts[None, :]).astype(jnp.int32), axis=0)
    padded = (counts + rows_per - 1) // rows_per * rows_per
    start = jnp.cumsum(counts) - counts
    pend = jnp.cumsum(padded)
    pstart = pend - padded
    steps = A // rows_per + N_EXPERTS + 1
    first_row = jnp.arange(steps + 1, dtype=jnp.int32) * rows_per
    block_e = jnp.minimum(
        jnp.sum((pend[None, :] <= first_row[:, None]).astype(jnp.int32), axis=1), N_EXPERTS - 1)
    r = jnp.arange((steps + 1) * rows_per, dtype=jnp.int32)
    e_r = jnp.repeat(block_e, rows_per)
    off = r - pstart[e_r]
    valid = off < counts[e_r]
    a = order[jnp.clip(start[e_r] + off, 0, A - 1)]
    tok = a // TOP_K
    src = jnp.where(valid, tok, 0).astype(jnp.int32) * TILE_ROWS
    spare = TOP_K * T + r % rows_per
    dst = jnp.where(valid, (a % TOP_K) * T + tok, spare).astype(jnp.int32)
    dst = jnp.concatenate([spare[:rows_per], dst[:-2 * rows_per]]).astype(jnp.int32) * TILE_ROWS
    n_used = (pend[-1] // rows_per).astype(jnp.int32).reshape(1)
    return src, dst, block_e[:steps].astype(jnp.int32), n_used


def _rotary_tables(positions):
    half = ROT_DIM // 2
    inv_freq = ROPE_THETA ** (-jnp.arange(0, ROT_DIM, 2, dtype=F32) / ROT_DIM)
    ang = positions.reshape(-1).astype(F32)[:, None] * inv_freq
    cos, sin = jnp.cos(ang), jnp.sin(ang)
    T = ang.shape[0]
    ones = jnp.ones((T, HEAD_DIM - ROT_DIM), F32)
    zeros = jnp.zeros((T, HEAD_DIM - ROT_DIM), F32)
    zh = jnp.zeros((T, half), F32)
    c = jnp.concatenate([cos, cos, ones], axis=1)
    sa = jnp.concatenate([-sin, zh, zeros], axis=1)
    sb = jnp.concatenate([zh, sin, zeros], axis=1)
    reps = LANES // HEAD_DIM
    return jnp.tile(c, (1, reps)), jnp.tile(sa, (1, reps)), jnp.tile(sb, (1, reps))


def kernel(x, positions, w_in, pool_w, pool_scale, w_pool_branch, w_attn_branch, lambda_q1, lambda_k1, lambda_q2, lambda_k2, subln_w, w_out, ln1_g, ln1_b, w_router, b_router, w_gu, b_gu, w_down, b_down, ln2_g, ln2_b):
    B, S, D = x.shape
    assert D == D_MODEL and S % ATTN_BLOCK == 0 and w_in.shape[0] == DEPTH
    T = B * S
    assert T % PROJ_ROWS == 0 and T % MIX_ROWS == 0 and T % COMBINE_ROWS == 0
    assert (T * TOP_K) % EXPERT_ROWS == 0
    rot_c, rot_sa, rot_sb = _rotary_tables(positions)
    xf = x.reshape(T, D)
    xb = xf.astype(BF16)
    o_qk = POOL_WIDTH
    o_v = o_qk + 2 * QK_WIDTH
    o_g = o_v + V_WIDTH
    for l in range(DEPTH):
        wl = w_in[l]
        wu = wl[:, :o_qk].astype(BF16)
        wq = wl[:, o_qk:o_qk + QK_WIDTH] * (HEAD_DIM ** -0.5)
        wqk = jnp.concatenate([wq, wl[:, o_qk + QK_WIDTH:o_v]], axis=1).astype(BF16)
        wv = wl[:, o_v:o_g].astype(BF16)
        wg = wl[:, o_g:].astype(BF16)
        u, qk, v = _projections(xb, wu, wqk, wv, rot_c, rot_sa, rot_sb)
        p = _pool(u, pool_w[l].astype(BF16), pool_scale[l].reshape(1, POOL_WIDTH), B, S)
        lambda_init = 0.8 - 0.6 * math.exp(-0.3 * l)
        lam = (jnp.exp(jnp.sum(lambda_q1[l] * lambda_k1[l]))
               - jnp.exp(jnp.sum(lambda_q2[l] * lambda_k2[l])) + lambda_init).reshape(1)
        o = _attention(qk, v, lam, subln_w[l].reshape(1, V_HEAD_DIM), B, S, 1.0 - lambda_init)
        wr = jnp.zeros((D, LANES), F32).at[:, :N_EXPERTS].set(w_router[l])
        wr_hi = wr.astype(BF16)
        wr_lo = (wr - wr_hi.astype(F32)).astype(BF16)
        br = jnp.full((1, LANES), NEG_BIG, F32).at[0, :N_EXPERTS].set(b_router[l])
        x1, idx, gate = _mix(
            xb, xf, p, o, wg, w_pool_branch[l].astype(BF16), w_attn_branch[l].astype(BF16),
            w_out[l].astype(BF16), ln1_g[l].reshape(1, D), ln1_b[l].reshape(1, D),
            wr_hi, wr_lo, br)
        src, dst, block_e, n_used = _routing_tables(idx[:, :TOP_K])
        ys = _expert_blocks(l, block_e, n_used, src, dst, x1, w_gu, b_gu, w_down, b_down)
        xf, xb = _combine(ys, gate, x1, ln2_g[l].reshape(1, D), ln2_b[l].reshape(1, D))
    return xf.reshape(B, S, D)
```

```python
import functools
import math

import jax
import jax.numpy as jnp
from jax import lax
from jax.experimental import pallas as pl
from jax.experimental.pallas import tpu as pltpu

F32 = jnp.float32
BF16 = jnp.bfloat16

D_MODEL = 1024
DEPTH = 2
POOL_WINDOWS = (2, 4, 8, 16)
POOL_GROUP_DIM = 128
POOL_WIDTH = POOL_GROUP_DIM * len(POOL_WINDOWS)
N_HEADS = 8
HEAD_DIM = 64
V_HEAD_DIM = 2 * HEAD_DIM
QK_WIDTH = N_HEADS * 2 * HEAD_DIM
V_WIDTH = N_HEADS * V_HEAD_DIM
ROPE_THETA = 500000.0
ROT_DIM = HEAD_DIM // 4
N_EXPERTS = 32
TOP_K = 4
D_FF = D_MODEL
SWIGLU_ALPHA = 1.702
SWIGLU_LIMIT = 7.0
LN_EPS = 1e-5
DEEPNORM_ALPHA = (2 * DEPTH) ** 0.25

LANES = 128
VMEM_LIMIT = 48 * 1024 * 1024

PROJ_ROWS = 512
ATTN_BLOCK = 512
MIX_ROWS = 256
EXPERT_ROWS = 256
COMBINE_ROWS = 256
NEG_BIG = -1e30


def _dot(a, b):
    return jnp.dot(a, b, preferred_element_type=F32)


def _layer_norm(z, g, b):
    mu = jnp.mean(z, axis=-1, keepdims=True)
    zc = z - mu
    var = jnp.mean(zc * zc, axis=-1, keepdims=True)
    return zc * lax.rsqrt(var + LN_EPS) * g + b


TILE_ROWS = D_MODEL // LANES


def _load_token_tiles(ref, tokens):
    return jnp.concatenate(
        [ref[pl.ds(s, tokens, stride=TILE_ROWS), :] for s in range(TILE_ROWS)], axis=1)


def _store_token_tiles(ref, value):
    tokens = value.shape[0]
    for s in range(TILE_ROWS):
        ref[pl.ds(s, tokens, stride=TILE_ROWS), :] = value[:, s * LANES:(s + 1) * LANES]


def _proj_kernel(xb_ref, wu_ref, wqk_ref, wv_ref, c_ref, sa_ref, sb_ref, u_ref, qk_ref, v_ref):
    xb = xb_ref[...]
    u_ref[...] = _dot(xb, wu_ref[...])
    c = c_ref[...]
    sa = sa_ref[...]
    sb = sb_ref[...]
    chunk = 4 * LANES
    for j in range(2 * QK_WIDTH // chunk):
        t = _dot(xb, wqk_ref[:, j * chunk:(j + 1) * chunk])
        for s in range(chunk // LANES):
            ts = t[:, s * LANES:(s + 1) * LANES]
            up = pltpu.roll(ts, LANES - ROT_DIM // 2, axis=1)
            dn = pltpu.roll(ts, ROT_DIM // 2, axis=1)
            lo = j * chunk + s * LANES
            qk_ref[:, lo:lo + LANES] = (ts * c + up * sa + dn * sb).astype(BF16)
    for j in range(V_WIDTH // chunk):
        v_ref[:, j * chunk:(j + 1) * chunk] = _dot(
            xb, wv_ref[:, j * chunk:(j + 1) * chunk]).astype(BF16)


def _projections(xb, wu, wqk, wv, rot_c, rot_sa, rot_sb):
    T = xb.shape[0]
    tm = PROJ_ROWS
    row = lambda i: (i, 0)
    fixed = lambda i: (0, 0)
    return pl.pallas_call(
        _proj_kernel,
        grid=(T // tm,),
        in_specs=[
            pl.BlockSpec((tm, D_MODEL), row),
            pl.BlockSpec((D_MODEL, POOL_WIDTH), fixed),
            pl.BlockSpec((D_MODEL, 2 * QK_WIDTH), fixed),
            pl.BlockSpec((D_MODEL, V_WIDTH), fixed),
            pl.BlockSpec((tm, LANES), row),
            pl.BlockSpec((tm, LANES), row),
            pl.BlockSpec((tm, LANES), row),
        ],
        out_specs=[
            pl.BlockSpec((tm, POOL_WIDTH), row),
            pl.BlockSpec((tm, 2 * QK_WIDTH), row),
            pl.BlockSpec((tm, V_WIDTH), row),
        ],
        out_shape=[
            jax.ShapeDtypeStruct((T, POOL_WIDTH), F32),
            jax.ShapeDtypeStruct((T, 2 * QK_WIDTH), BF16),
            jax.ShapeDtypeStruct((T, V_WIDTH), BF16),
        ],
        compiler_params=pltpu.CompilerParams(
            dimension_semantics=("parallel",), vmem_limit_bytes=VMEM_LIMIT),
        name="projections",
    )(xb, wu, wqk, wv, rot_c, rot_sa, rot_sb)


def _pool_kernel(u_ref, w_ref, sc_ref, p_ref):
    S = u_ref.shape[0]
    row = lax.broadcasted_iota(jnp.int32, (S, POOL_GROUP_DIM), 0)
    for g, window in enumerate(POOL_WINDOWS):
        cols = slice(g * POOL_GROUP_DIM, (g + 1) * POOL_GROUP_DIM)
        u = u_ref[:, cols]
        acc = u
        span = 1
        while span < window:
            shifted = jnp.where(row >= span, pltpu.roll(acc, span, axis=0), 0.0)
            acc = acc + shifted
            span *= 2
        count = jnp.minimum(row + 1, window).astype(F32)
        d = (acc / count - u).astype(BF16)
        y = _dot(d, w_ref[g]) * sc_ref[:, cols]
        p_ref[:, cols] = y.astype(BF16)


def _pool(u, pool_w, pool_scale, B, S):
    T = u.shape[0]
    return pl.pallas_call(
        _pool_kernel,
        grid=(B,),
        in_specs=[
            pl.BlockSpec((S, POOL_WIDTH), lambda b: (b, 0)),
            pl.BlockSpec(pool_w.shape, lambda b: (0, 0, 0)),
            pl.BlockSpec((1, POOL_WIDTH), lambda b: (0, 0)),
        ],
        out_specs=pl.BlockSpec((S, POOL_WIDTH), lambda b: (b, 0)),
        out_shape=jax.ShapeDtypeStruct((T, POOL_WIDTH), BF16),
        compiler_params=pltpu.CompilerParams(
            dimension_semantics=("parallel",), vmem_limit_bytes=VMEM_LIMIT),
        name="pool",
    )(u, pool_w, pool_scale)


def _attn_kernel(lam_ref, q_ref, k_ref, v_ref, w_ref, o_ref, vt_ref, m_ref, l_ref, acc_ref,
                 s_ref, cmax_ref, *, post_scale):
    tq = q_ref.shape[0]
    qi = pl.program_id(2)

    @pl.when(qi == 0)
    def _():
        for c in range(vt_ref.shape[0]):
            vt_ref[c] = v_ref[c * tq:(c + 1) * tq, :].astype(F32).T.astype(BF16)

    q = q_ref[...]
    lane = lax.broadcasted_iota(jnp.int32, q.shape, 1)
    zero = jnp.zeros_like(q)
    q2 = jnp.concatenate(
        [jnp.where(lane < HEAD_DIM, q, zero), jnp.where(lane >= HEAD_DIM, q, zero)], axis=0)

    m_ref[...] = jnp.full(m_ref.shape, -jnp.inf, F32)
    l_ref[...] = jnp.zeros(l_ref.shape, F32)
    acc_ref[...] = jnp.zeros(acc_ref.shape, F32)

    def scores(j, diagonal, slot):
        ks = k_ref[pl.ds(pl.multiple_of(j * tq, tq), tq), :]
        s = lax.dot_general(ks, q2, (((1,), (1,)), ((), ())), preferred_element_type=F32)
        if diagonal:
            key = lax.broadcasted_iota(jnp.int32, s.shape, 0)
            qry = lax.broadcasted_iota(jnp.int32, s.shape, 1)
            qry = jnp.where(qry >= tq, qry - tq, qry)
            s = jnp.where(key <= qry, s, -jnp.inf)
        s_ref[slot] = s
        cmax_ref[slot] = jnp.max(s, axis=0, keepdims=True)

    def update(j, slot):
        m = m_ref[...]
        m_new = jnp.maximum(m, cmax_ref[slot])
        rescale = jnp.exp(m - m_new)
        p = jnp.exp(s_ref[slot] - m_new)
        m_ref[...] = m_new
        l_ref[...] = rescale * l_ref[...] + jnp.sum(p, axis=0, keepdims=True)
        acc_ref[...] = rescale * acc_ref[...] + _dot(vt_ref[j], p.astype(BF16))

    @pl.when(qi == 0)
    def _():
        scores(0, True, 0)
        update(0, 0)

    @pl.when(qi > 0)
    def _():
        scores(0, False, 0)

        def body(j, carry):
            scores(j + 1, False, (j + 1) % 2)
            update(j, j % 2)
            return carry

        lax.fori_loop(0, qi - 1, body, 0)
        scores(qi, True, qi % 2)
        update(qi - 1, (qi - 1) % 2)
        update(qi, qi % 2)

    o = acc_ref[...] / l_ref[...]
    od = (o[:, :tq] - lam_ref[0] * o[:, tq:]).T
    od = od * lax.rsqrt(jnp.mean(od * od, axis=-1, keepdims=True) + LN_EPS)
    o_ref[...] = (od * w_ref[...] * post_scale).astype(BF16)


def _attention(qk, v, lam, subln_w, B, S, post_scale):
    T = qk.shape[0]
    tq = ATTN_BLOCK
    nq = S // tq
    return pl.pallas_call(
        functools.partial(_attn_kernel, post_scale=post_scale),
        grid=(B, N_HEADS, nq),
        in_specs=[
            pl.BlockSpec(memory_space=pltpu.SMEM),
            pl.BlockSpec((tq, V_HEAD_DIM), lambda b, h, i: (b * nq + i, h)),
            pl.BlockSpec((S, V_HEAD_DIM), lambda b, h, i: (b, N_HEADS + h)),
            pl.BlockSpec((S, V_HEAD_DIM), lambda b, h, i: (b, h)),
            pl.BlockSpec((1, V_HEAD_DIM), lambda b, h, i: (0, 0)),
        ],
        out_specs=pl.BlockSpec((tq, V_HEAD_DIM), lambda b, h, i: (b * nq + i, h)),
        out_shape=jax.ShapeDtypeStruct((T, V_WIDTH), BF16),
        scratch_shapes=[
            pltpu.VMEM((nq, V_HEAD_DIM, tq), BF16),
            pltpu.VMEM((1, 2 * tq), F32),
            pltpu.VMEM((1, 2 * tq), F32),
            pltpu.VMEM((V_HEAD_DIM, 2 * tq), F32),
            pltpu.VMEM((2, tq, 2 * tq), F32),
            pltpu.VMEM((2, 1, 2 * tq), F32),
        ],
        compiler_params=pltpu.CompilerParams(
            dimension_semantics=("parallel", "parallel", "arbitrary"),
            vmem_limit_bytes=VMEM_LIMIT),
        name="diff_attention",
    )(lam, qk, qk, v, subln_w)


def _mix_kernel(xb_ref, x_ref, p_ref, o_ref, wg_ref, wpb_ref, wab_ref, wout_ref, g_ref, b_ref,
                wrh_ref, wrl_ref, br_ref, x1_ref, idx_ref, gate_ref):
    xb = xb_ref[...]
    merged = jax.nn.sigmoid(_dot(xb, wg_ref[:, :D_MODEL])) * _dot(p_ref[...], wpb_ref[...])
    merged = merged + jax.nn.sigmoid(_dot(xb, wg_ref[:, D_MODEL:])) * _dot(o_ref[...], wab_ref[...])
    mix = _dot(merged.astype(BF16), wout_ref[...])
    x1 = _layer_norm(DEEPNORM_ALPHA * x_ref[...] + mix, g_ref[...], b_ref[...])
    _store_token_tiles(x1_ref, x1)

    hi = x1.astype(BF16)
    lo = (x1 - hi.astype(F32)).astype(BF16)
    logits = _dot(hi, wrh_ref[...]) + _dot(lo, wrh_ref[...]) + _dot(hi, wrl_ref[...]) + br_ref[...]

    lane = lax.broadcasted_iota(jnp.int32, logits.shape, 1)
    lane_f = lane.astype(F32)
    work = logits
    vals, idxs = [], []
    for _ in range(TOP_K):
        top = jnp.max(work, axis=1, keepdims=True)
        first = jnp.min(jnp.where(work == top, lane_f, float(LANES)), axis=1, keepdims=True)
        vals.append(top)
        idxs.append(first)
        work = jnp.where(lane_f == first, -jnp.inf, work)
    exps = [jnp.exp(v - vals[0]) for v in vals]
    denom = exps[0] + exps[1] + exps[2] + exps[3]
    idx_out = jnp.zeros(logits.shape, F32)
    gate_out = jnp.zeros(logits.shape, F32)
    for k in range(TOP_K):
        idx_out = jnp.where(lane == k, idxs[k], idx_out)
        gate_out = jnp.where(lane == k, exps[k] / denom, gate_out)
    idx_ref[...] = idx_out.astype(jnp.int32)
    gate_ref[...] = gate_out


def _mix(xb, x, p, o, wg, wpb, wab, wout, ln_g, ln_b, wr_hi, wr_lo, br):
    T = x.shape[0]
    tm = MIX_ROWS
    row = lambda i: (i, 0)
    fixed = lambda i: (0, 0)
    return pl.pallas_call(
        _mix_kernel,
        grid=(T // tm,),
        in_specs=[
            pl.BlockSpec((tm, D_MODEL), row),
            pl.BlockSpec((tm, D_MODEL), row),
            pl.BlockSpec((tm, POOL_WIDTH), row),
            pl.BlockSpec((tm, V_WIDTH), row),
            pl.BlockSpec(wg.shape, fixed),
            pl.BlockSpec(wpb.shape, fixed),
            pl.BlockSpec(wab.shape, fixed),
            pl.BlockSpec(wout.shape, fixed),
            pl.BlockSpec((1, D_MODEL), fixed),
            pl.BlockSpec((1, D_MODEL), fixed),
            pl.BlockSpec((D_MODEL, LANES), fixed),
            pl.BlockSpec((D_MODEL, LANES), fixed),
            pl.BlockSpec((1, LANES), fixed),
        ],
        out_specs=[
            pl.BlockSpec((tm * TILE_ROWS, LANES), row),
            pl.BlockSpec((tm, LANES), row),
            pl.BlockSpec((tm, LANES), row),
        ],
        out_shape=[
            jax.ShapeDtypeStruct((T * TILE_ROWS, LANES), F32),
            jax.ShapeDtypeStruct((T, LANES), jnp.int32),
            jax.ShapeDtypeStruct((T, LANES), F32),
        ],
        compiler_params=pltpu.CompilerParams(
            dimension_semantics=("parallel",), vmem_limit_bytes=VMEM_LIMIT),
        name="mixer_out_router",
    )(xb, x, p, o, wg, wpb, wab, wout, ln_g, ln_b, wr_hi, wr_lo, br)


def _tile_copy(src, dst, src_row, dst_row, sem):
    return pltpu.make_async_copy(
        src.at[pl.ds(src_row, TILE_ROWS)], dst.at[pl.ds(dst_row, TILE_ROWS)], sem)


def _gmm_kernel(be_ref, nu_ref, src_b0, src_b1, src_n2, dst_prv, x_hbm, wgu_ref, bgu_ref, wd_ref,
                bd_ref, y_hbm, xbuf, ybuf, wgu_b, wd_b, gsem, ssem):
    i = pl.program_id(0)
    n_used = nu_ref[0]
    rows = xbuf.shape[1] // TILE_ROWS
    x_cur, x_n1, x_n2 = i % 3, (i + 1) % 3, (i + 2) % 3
    y_cur = i % 2
    y_prv = 1 - y_cur
    n_chunks = 4
    cw = D_FF // n_chunks
    spare_b = y_hbm.shape[0] - rows * TILE_ROWS

    def wait_gather(slot, count):
        n = count * TILE_ROWS
        pltpu.make_async_copy(x_hbm.at[pl.ds(0, n)], xbuf.at[slot, pl.ds(0, n)],
                              gsem.at[slot]).wait()

    def wait_scatter(slot, count):
        n = count * TILE_ROWS
        pltpu.make_async_copy(ybuf.at[slot, pl.ds(0, n)], y_hbm.at[pl.ds(0, n)],
                              ssem.at[slot]).wait()

    def tile_row(r):
        return r * TILE_ROWS if isinstance(r, int) else pl.multiple_of(r * TILE_ROWS, TILE_ROWS)

    def gather_row(table, slot, r):
        src_row = pl.multiple_of(table[0, 0, r], TILE_ROWS)
        _tile_copy(x_hbm, xbuf.at[slot], src_row, tile_row(r), gsem.at[slot]).start(priority=0)

    def scatter_row(slot, r, dst_row):
        _tile_copy(ybuf.at[slot], y_hbm, tile_row(r), dst_row, ssem.at[slot]).start(priority=1)

    @pl.when(i == 0)
    def _():
        ybuf[...] = jnp.zeros(ybuf.shape, F32)

        def issue(r, c):
            gather_row(src_b0, 0, r)
            gather_row(src_b1, 1, r)
            scatter_row(0, r, spare_b + tile_row(r))
            return c

        lax.fori_loop(0, rows, issue, 0)
        wait_gather(0, rows)

    @pl.when((i < n_used) & ((i == 0) | (be_ref[i] != be_ref[jnp.maximum(i - 1, 0)])))
    def _():
        step = 128

        def cast(c, carry):
            sl = pl.ds(pl.multiple_of(c * step, step), step)
            wgu_b[sl, :] = wgu_ref[sl, :].astype(BF16)
            wd_b[sl, :] = wd_ref[sl, :].astype(BF16)
            return carry

        lax.fori_loop(0, D_MODEL // step, cast, 0)

    @pl.when(i < n_used)
    def _():
        xb = _load_token_tiles(xbuf.at[x_cur], rows).astype(BF16)
        y = jnp.zeros((rows, D_MODEL), F32) + bd_ref[...]
        half = rows // 2
        for c in range(n_chunks):
            if c in (0, 1):
                for r in range(c * half, (c + 1) * half):
                    gather_row(src_n2, x_n2, r)
            if c in (1, 2):
                for r in range((c - 1) * half, c * half):
                    scatter_row(y_prv, r, pl.multiple_of(dst_prv[0, 0, r], TILE_ROWS))
            g_cols = slice(c * cw, (c + 1) * cw)
            u_cols = slice(D_FF + c * cw, D_FF + (c + 1) * cw)
            hg = _dot(xb, wgu_b[:, g_cols]) + bgu_ref[:, g_cols]
            hu = _dot(xb, wgu_b[:, u_cols]) + bgu_ref[:, u_cols]
            hg = jnp.minimum(hg, SWIGLU_LIMIT)
            hu = jnp.clip(hu, -SWIGLU_LIMIT, SWIGLU_LIMIT)
            act = (hu + 1.0) * hg * jax.nn.sigmoid(SWIGLU_ALPHA * hg)
            y = y + _dot(act.astype(BF16), wd_b[g_cols, :])
            if c == 1:
                wait_gather(x_n1, rows)
            if c == 2:
                wait_scatter(y_cur, rows)
        _store_token_tiles(ybuf.at[y_cur], y)

    @pl.when(i == n_used)
    def _():
        def issue(r, c):
            scatter_row(y_prv, r, pl.multiple_of(dst_prv[0, 0, r], TILE_ROWS))
            return c

        lax.fori_loop(0, rows, issue, 0)
        wait_gather(x_n1, rows)
        wait_scatter(y_cur, rows)
        wait_scatter(y_prv, rows)


def _expert_blocks(layer, block_e, n_used, src, dst, x1, w_gu, b_gu, w_down, b_down):
    T = x1.shape[0] // TILE_ROWS
    rows = EXPERT_ROWS
    steps = block_e.shape[0]
    depth = w_gu.shape[0]
    src = src.reshape(steps + 1, 1, rows)
    grid_spec = pltpu.PrefetchScalarGridSpec(
        num_scalar_prefetch=2,
        grid=(steps,),
        in_specs=[
            pl.BlockSpec((1, 1, rows), lambda i, be, nu: (0, 0, 0), memory_space=pltpu.SMEM),
            pl.BlockSpec((1, 1, rows), lambda i, be, nu: (1, 0, 0), memory_space=pltpu.SMEM),
            pl.BlockSpec((1, 1, rows), lambda i, be, nu: (jnp.minimum(i + 2, steps), 0, 0),
                         memory_space=pltpu.SMEM),
            pl.BlockSpec((1, 1, rows), lambda i, be, nu: (i, 0, 0), memory_space=pltpu.SMEM),
            pl.BlockSpec(memory_space=pl.ANY),
            pl.BlockSpec((None, None, D_MODEL, 2 * D_FF), lambda i, be, nu: (layer, be[i], 0, 0)),
            pl.BlockSpec((None, None, 1, 2 * D_FF), lambda i, be, nu: (layer, be[i], 0, 0)),
            pl.BlockSpec((None, None, D_FF, D_MODEL), lambda i, be, nu: (layer, be[i], 0, 0)),
            pl.BlockSpec((None, None, 1, D_MODEL), lambda i, be, nu: (layer, be[i], 0, 0)),
        ],
        out_specs=pl.BlockSpec(memory_space=pl.ANY),
        scratch_shapes=[
            pltpu.VMEM((3, rows * TILE_ROWS, LANES), F32),
            pltpu.VMEM((2, rows * TILE_ROWS, LANES), F32),
            pltpu.VMEM((D_MODEL, 2 * D_FF), BF16),
            pltpu.VMEM((D_FF, D_MODEL), BF16),
            pltpu.SemaphoreType.DMA((3,)),
            pltpu.SemaphoreType.DMA((2,)),
        ],
    )
    return pl.pallas_call(
        _gmm_kernel,
        grid_spec=grid_spec,
        out_shape=jax.ShapeDtypeStruct(((TOP_K * T + 2 * rows) * TILE_ROWS, LANES), F32),
        compiler_params=pltpu.CompilerParams(
            dimension_semantics=("arbitrary",), vmem_limit_bytes=VMEM_LIMIT),
        name="expert_blocks",
    )(block_e, n_used, src, src, src, dst.reshape(steps, 1, rows), x1, w_gu,
      b_gu.reshape(depth, N_EXPERTS, 1, 2 * D_FF), w_down,
      b_down.reshape(depth, N_EXPERTS, 1, D_MODEL))


def _combine_kernel(y0_ref, y1_ref, y2_ref, y3_ref, gate_ref, x1_ref, g_ref, b_ref, x2_ref, xb2_ref):
    tm = x2_ref.shape[0]
    gate = gate_ref[...]
    z = DEEPNORM_ALPHA * _load_token_tiles(x1_ref, tm)
    for k, y_ref in enumerate((y0_ref, y1_ref, y2_ref, y3_ref)):
        z = z + gate[:, k:k + 1] * _load_token_tiles(y_ref, tm)
    x2 = _layer_norm(z, g_ref[...], b_ref[...])
    x2_ref[...] = x2
    xb2_ref[...] = x2.astype(BF16)


def _combine(ys, gate, x1, ln_g, ln_b):
    T = x1.shape[0] // TILE_ROWS
    tm = COMBINE_ROWS
    nt = T // tm
    row = lambda i: (i, 0)
    fixed = lambda i: (0, 0)
    y_specs = [pl.BlockSpec((tm * TILE_ROWS, LANES),
                            functools.partial(lambda k, i: (k * nt + i, 0), k))
               for k in range(TOP_K)]
    return pl.pallas_call(
        _combine_kernel,
        grid=(nt,),
        in_specs=y_specs + [
            pl.BlockSpec((tm, LANES), row),
            pl.BlockSpec((tm * TILE_ROWS, LANES), row),
            pl.BlockSpec((1, D_MODEL), fixed),
            pl.BlockSpec((1, D_MODEL), fixed),
        ],
        out_specs=[
            pl.BlockSpec((tm, D_MODEL), row),
            pl.BlockSpec((tm, D_MODEL), row),
        ],
        out_shape=[
            jax.ShapeDtypeStruct((T, D_MODEL), F32),
            jax.ShapeDtypeStruct((T, D_MODEL), BF16),
        ],
        compiler_params=pltpu.CompilerParams(
            dimension_semantics=("parallel",), vmem_limit_bytes=VMEM_LIMIT),
        name="combine_ln",
    )(ys, ys, ys, ys, gate, x1, ln_g, ln_b)


def _routing_tables(top_idx):
    T = top_idx.shape[0]
    A = T * TOP_K
    rows_per = EXPERT_ROWS
    flat_e = top_idx.reshape(A)
    key = jnp.sort(flat_e * A + jnp.arange(A, dtype=jnp.int32))
    order = key % A
    experts = jnp.arange(N_EXPERTS, dtype=jnp.int32)
    counts = jnp.sum((flat_e[:, None] == experts[None, :]).astype(jnp.int32), axis=0)
    padded = (counts + rows_per - 1) // rows_per * rows_per
    start = jnp.cumsum(counts) - counts
    pend = jnp.cumsum(padded)
    pstart = pend - padded
    steps = A // rows_per + N_EXPERTS + 1
    first_row = jnp.arange(steps + 1, dtype=jnp.int32) * rows_per
    block_e = jnp.minimum(
        jnp.sum((pend[None, :] <= first_row[:, None]).astype(jnp.int32), axis=1), N_EXPERTS - 1)
    r = jnp.arange((steps + 1) * rows_per, dtype=jnp.int32)
    e_r = jnp.repeat(block_e, rows_per)
    off = r - pstart[e_r]
    valid = off < counts[e_r]
    a = order[jnp.clip(start[e_r] + off, 0, A - 1)]
    tok = a // TOP_K
    src = jnp.where(valid, tok, 0).astype(jnp.int32) * TILE_ROWS
    spare = TOP_K * T + r % rows_per
    dst = jnp.where(valid, (a % TOP_K) * T + tok, spare).astype(jnp.int32)
    dst = jnp.concatenate([spare[:rows_per], dst[:-2 * rows_per]]).astype(jnp.int32) * TILE_ROWS
    n_used = (pend[-1] // rows_per).astype(jnp.int32).reshape(1)
    return src, dst, block_e[:steps].astype(jnp.int32), n_used


def _rotary_tables(positions):
    half = ROT_DIM // 2
    inv_freq = ROPE_THETA ** (-jnp.arange(0, ROT_DIM, 2, dtype=F32) / ROT_DIM)
    ang = positions.reshape(-1).astype(F32)[:, None] * inv_freq
    cos, sin = jnp.cos(ang), jnp.sin(ang)
    T = ang.shape[0]
    ones = jnp.ones((T, HEAD_DIM - ROT_DIM), F32)
    zeros = jnp.zeros((T, HEAD_DIM - ROT_DIM), F32)
    zh = jnp.zeros((T, half), F32)
    c = jnp.concatenate([cos, cos, ones], axis=1)
    sa = jnp.concatenate([-sin, zh, zeros], axis=1)
    sb = jnp.concatenate([zh, sin, zeros], axis=1)
    reps = LANES // HEAD_DIM
    return jnp.tile(c, (1, reps)), jnp.tile(sa, (1, reps)), jnp.tile(sb, (1, reps))


def kernel(x, positions, w_in, pool_w, pool_scale, w_pool_branch, w_attn_branch, lambda_q1, lambda_k1, lambda_q2, lambda_k2, subln_w, w_out, ln1_g, ln1_b, w_router, b_router, w_gu, b_gu, w_down, b_down, ln2_g, ln2_b):
    B, S, D = x.shape
    assert D == D_MODEL and S % ATTN_BLOCK == 0 and w_in.shape[0] == DEPTH
    T = B * S
    assert T % PROJ_ROWS == 0 and T % MIX_ROWS == 0 and T % COMBINE_ROWS == 0
    assert (T * TOP_K) % EXPERT_ROWS == 0
    rot_c, rot_sa, rot_sb = _rotary_tables(positions)
    xf = x.reshape(T, D)
    xb = xf.astype(BF16)
    o_qk = POOL_WIDTH
    o_v = o_qk + 2 * QK_WIDTH
    o_g = o_v + V_WIDTH
    for l in range(DEPTH):
        wl = w_in[l]
        wu = wl[:, :o_qk].astype(BF16)
        wq = wl[:, o_qk:o_qk + QK_WIDTH] * (HEAD_DIM ** -0.5)
        wqk = jnp.concatenate([wq, wl[:, o_qk + QK_WIDTH:o_v]], axis=1).astype(BF16)
        wv = wl[:, o_v:o_g].astype(BF16)
        wg = wl[:, o_g:].astype(BF16)
        u, qk, v = _projections(xb, wu, wqk, wv, rot_c, rot_sa, rot_sb)
        p = _pool(u, pool_w[l].astype(BF16), pool_scale[l].reshape(1, POOL_WIDTH), B, S)
        lambda_init = 0.8 - 0.6 * math.exp(-0.3 * l)
        lam = (jnp.exp(jnp.sum(lambda_q1[l] * lambda_k1[l]))
               - jnp.exp(jnp.sum(lambda_q2[l] * lambda_k2[l])) + lambda_init).reshape(1)
        o = _attention(qk, v, lam, subln_w[l].reshape(1, V_HEAD_DIM), B, S, 1.0 - lambda_init)
        wr = jnp.zeros((D, LANES), F32).at[:, :N_EXPERTS].set(w_router[l])
        wr_hi = wr.astype(BF16)
        wr_lo = (wr - wr_hi.astype(F32)).astype(BF16)
        br = jnp.full((1, LANES), NEG_BIG, F32).at[0, :N_EXPERTS].set(b_router[l])
        x1, idx, gate = _mix(
            xb, xf, p, o, wg, w_pool_branch[l].astype(BF16), w_attn_branch[l].astype(BF16),
            w_out[l].astype(BF16), ln1_g[l].reshape(1, D), ln1_b[l].reshape(1, D),
            wr_hi, wr_lo, br)
        src, dst, block_e, n_used = _routing_tables(idx[:, :TOP_K])
        ys = _expert_blocks(l, block_e, n_used, src, dst, x1, w_gu, b_gu, w_down, b_down)
        xf, xb = _combine(ys, gate, x1, ln2_g[l].reshape(1, D), ln2_b[l].reshape(1, D))
    return xf.reshape(B, S, D)
```

```python
import functools
import math

import jax
import jax.numpy as jnp
from jax import lax
from jax.experimental import pallas as pl
from jax.experimental.pallas import tpu as pltpu

F32 = jnp.float32
BF16 = jnp.bfloat16

D_MODEL = 1024
DEPTH = 2
POOL_WINDOWS = (2, 4, 8, 16)
POOL_GROUP_DIM = 128
POOL_WIDTH = POOL_GROUP_DIM * len(POOL_WINDOWS)
N_HEADS = 8
HEAD_DIM = 64
V_HEAD_DIM = 2 * HEAD_DIM
QK_WIDTH = N_HEADS * 2 * HEAD_DIM
V_WIDTH = N_HEADS * V_HEAD_DIM
ROPE_THETA = 500000.0
ROT_DIM = HEAD_DIM // 4
N_EXPERTS = 32
TOP_K = 4
D_FF = D_MODEL
SWIGLU_ALPHA = 1.702
SWIGLU_LIMIT = 7.0
LN_EPS = 1e-5
DEEPNORM_ALPHA = (2 * DEPTH) ** 0.25

LANES = 128
VMEM_LIMIT = 48 * 1024 * 1024

PROJ_ROWS = 512
ATTN_BLOCK = 512
MIX_ROWS = 256
EXPERT_ROWS = 256
COMBINE_ROWS = 256
NEG_BIG = -1e30


def _dot(a, b):
    return jnp.dot(a, b, preferred_element_type=F32)


def _layer_norm(z, g, b):
    mu = jnp.mean(z, axis=-1, keepdims=True)
    zc = z - mu
    var = jnp.mean(zc * zc, axis=-1, keepdims=True)
    return zc * lax.rsqrt(var + LN_EPS) * g + b


TILE_ROWS = D_MODEL // LANES


def _load_token_tiles(ref, tokens):
    return jnp.concatenate(
        [ref[pl.ds(s, tokens, stride=TILE_ROWS), :] for s in range(TILE_ROWS)], axis=1)


def _store_token_tiles(ref, value):
    tokens = value.shape[0]
    for s in range(TILE_ROWS):
        ref[pl.ds(s, tokens, stride=TILE_ROWS), :] = value[:, s * LANES:(s + 1) * LANES]


def _proj_kernel(xb_ref, wu_ref, wqk_ref, wv_ref, c_ref, sa_ref, sb_ref, u_ref, qk_ref, v_ref):
    xb = xb_ref[...]
    u_ref[...] = _dot(xb, wu_ref[...])
    c = c_ref[...]
    sa = sa_ref[...]
    sb = sb_ref[...]
    chunk = 4 * LANES
    for j in range(2 * QK_WIDTH // chunk):
        t = _dot(xb, wqk_ref[:, j * chunk:(j + 1) * chunk])
        for s in range(chunk // LANES):
            ts = t[:, s * LANES:(s + 1) * LANES]
            up = pltpu.roll(ts, LANES - ROT_DIM // 2, axis=1)
            dn = pltpu.roll(ts, ROT_DIM // 2, axis=1)
            lo = j * chunk + s * LANES
            qk_ref[:, lo:lo + LANES] = (ts * c + up * sa + dn * sb).astype(BF16)
    for j in range(V_WIDTH // chunk):
        v_ref[:, j * chunk:(j + 1) * chunk] = _dot(
            xb, wv_ref[:, j * chunk:(j + 1) * chunk]).astype(BF16)


def _projections(xb, wu, wqk, wv, rot_c, rot_sa, rot_sb):
    T = xb.shape[0]
    tm = PROJ_ROWS
    row = lambda i: (i, 0)
    fixed = lambda i: (0, 0)
    return pl.pallas_call(
        _proj_kernel,
        grid=(T // tm,),
        in_specs=[
            pl.BlockSpec((tm, D_MODEL), row),
            pl.BlockSpec((D_MODEL, POOL_WIDTH), fixed),
            pl.BlockSpec((D_MODEL, 2 * QK_WIDTH), fixed),
            pl.BlockSpec((D_MODEL, V_WIDTH), fixed),
            pl.BlockSpec((tm, LANES), row),
            pl.BlockSpec((tm, LANES), row),
            pl.BlockSpec((tm, LANES), row),
        ],
        out_specs=[
            pl.BlockSpec((tm, POOL_WIDTH), row),
            pl.BlockSpec((tm, 2 * QK_WIDTH), row),
            pl.BlockSpec((tm, V_WIDTH), row),
        ],
        out_shape=[
            jax.ShapeDtypeStruct((T, POOL_WIDTH), F32),
            jax.ShapeDtypeStruct((T, 2 * QK_WIDTH), BF16),
            jax.ShapeDtypeStruct((T, V_WIDTH), BF16),
        ],
        compiler_params=pltpu.CompilerParams(
            dimension_semantics=("parallel",), vmem_limit_bytes=VMEM_LIMIT),
        name="projections",
    )(xb, wu, wqk, wv, rot_c, rot_sa, rot_sb)


def _pool_kernel(u_ref, w_ref, sc_ref, p_ref):
    S = u_ref.shape[0]
    row = lax.broadcasted_iota(jnp.int32, (S, POOL_GROUP_DIM), 0)
    for g, window in enumerate(POOL_WINDOWS):
        cols = slice(g * POOL_GROUP_DIM, (g + 1) * POOL_GROUP_DIM)
        u = u_ref[:, cols]
        acc = u
        span = 1
        while span < window:
            shifted = jnp.where(row >= span, pltpu.roll(acc, span, axis=0), 0.0)
            acc = acc + shifted
            span *= 2
        count = jnp.minimum(row + 1, window).astype(F32)
        d = (acc / count - u).astype(BF16)
        y = _dot(d, w_ref[g]) * sc_ref[:, cols]
        p_ref[:, cols] = y.astype(BF16)


def _pool(u, pool_w, pool_scale, B, S):
    T = u.shape[0]
    return pl.pallas_call(
        _pool_kernel,
        grid=(B,),
        in_specs=[
            pl.BlockSpec((S, POOL_WIDTH), lambda b: (b, 0)),
            pl.BlockSpec(pool_w.shape, lambda b: (0, 0, 0)),
            pl.BlockSpec((1, POOL_WIDTH), lambda b: (0, 0)),
        ],
        out_specs=pl.BlockSpec((S, POOL_WIDTH), lambda b: (b, 0)),
        out_shape=jax.ShapeDtypeStruct((T, POOL_WIDTH), BF16),
        compiler_params=pltpu.CompilerParams(
            dimension_semantics=("parallel",), vmem_limit_bytes=VMEM_LIMIT),
        name="pool",
    )(u, pool_w, pool_scale)


def _attn_kernel(lam_ref, q_ref, k_ref, v_ref, w_ref, o_ref, m_ref, l_ref, acc_ref, *,
                 post_scale, tq):
    nq = q_ref.shape[0] // tq
    lane = lax.broadcasted_iota(jnp.int32, (tq, V_HEAD_DIM), 1)
    m_ref[...] = jnp.full(m_ref.shape, -jnp.inf, F32)
    l_ref[...] = jnp.zeros(l_ref.shape, F32)
    acc_ref[...] = jnp.zeros(acc_ref.shape, F32)

    def stacked_queries(qi):
        q = q_ref[qi * tq:(qi + 1) * tq, :]
        zero = jnp.zeros_like(q)
        return jnp.concatenate(
            [jnp.where(lane < HEAD_DIM, q, zero), jnp.where(lane >= HEAD_DIM, q, zero)], axis=0)

    for j in range(nq):
        ks = k_ref[j * tq:(j + 1) * tq, :]
        vt = v_ref[j * tq:(j + 1) * tq, :].astype(F32).T.astype(BF16)
        for qi in range(j, nq):
            s = lax.dot_general(ks, stacked_queries(qi), (((1,), (1,)), ((), ())),
                                preferred_element_type=F32)
            if qi == j:
                key = lax.broadcasted_iota(jnp.int32, s.shape, 0)
                qry = lax.broadcasted_iota(jnp.int32, s.shape, 1)
                qry = jnp.where(qry >= tq, qry - tq, qry)
                s = jnp.where(key <= qry, s, -jnp.inf)
            m = m_ref[qi]
            m_new = jnp.maximum(m, jnp.max(s, axis=0, keepdims=True))
            rescale = jnp.exp(m - m_new)
            p = jnp.exp(s - m_new)
            m_ref[qi] = m_new
            l_ref[qi] = rescale * l_ref[qi] + jnp.sum(p, axis=0, keepdims=True)
            acc_ref[qi] = rescale * acc_ref[qi] + _dot(vt, p.astype(BF16))
        o = acc_ref[j] / l_ref[j]
        od = (o[:, :tq] - lam_ref[0] * o[:, tq:]).T
        od = od * lax.rsqrt(jnp.mean(od * od, axis=-1, keepdims=True) + LN_EPS)
        o_ref[j * tq:(j + 1) * tq, :] = (od * w_ref[...] * post_scale).astype(BF16)


def _attention(qk, v, lam, subln_w, B, S, post_scale):
    T = qk.shape[0]
    tq = ATTN_BLOCK
    nq = S // tq
    head = lambda b, h: (b, h)
    return pl.pallas_call(
        functools.partial(_attn_kernel, post_scale=post_scale, tq=tq),
        grid=(B, N_HEADS),
        in_specs=[
            pl.BlockSpec(memory_space=pltpu.SMEM),
            pl.BlockSpec((S, V_HEAD_DIM), head),
            pl.BlockSpec((S, V_HEAD_DIM), lambda b, h: (b, N_HEADS + h)),
            pl.BlockSpec((S, V_HEAD_DIM), head),
            pl.BlockSpec((1, V_HEAD_DIM), lambda b, h: (0, 0)),
        ],
        out_specs=pl.BlockSpec((S, V_HEAD_DIM), head),
        out_shape=jax.ShapeDtypeStruct((T, V_WIDTH), BF16),
        scratch_shapes=[
            pltpu.VMEM((nq, 1, 2 * tq), F32),
            pltpu.VMEM((nq, 1, 2 * tq), F32),
            pltpu.VMEM((nq, V_HEAD_DIM, 2 * tq), F32),
        ],
        compiler_params=pltpu.CompilerParams(
            dimension_semantics=("parallel", "parallel"), vmem_limit_bytes=VMEM_LIMIT),
        name="diff_attention",
    )(lam, qk, qk, v, subln_w)


def _mix_kernel(xb_ref, x_ref, p_ref, o_ref, wg_ref, wpb_ref, wab_ref, wout_ref, g_ref, b_ref,
                wrh_ref, wrl_ref, br_ref, x1_ref, idx_ref, gate_ref):
    xb = xb_ref[...]
    merged = jax.nn.sigmoid(_dot(xb, wg_ref[:, :D_MODEL])) * _dot(p_ref[...], wpb_ref[...])
    merged = merged + jax.nn.sigmoid(_dot(xb, wg_ref[:, D_MODEL:])) * _dot(o_ref[...], wab_ref[...])
    mix = _dot(merged.astype(BF16), wout_ref[...])
    x1 = _layer_norm(DEEPNORM_ALPHA * x_ref[...] + mix, g_ref[...], b_ref[...])
    _store_token_tiles(x1_ref, x1)

    hi = x1.astype(BF16)
    lo = (x1 - hi.astype(F32)).astype(BF16)
    logits = _dot(hi, wrh_ref[...]) + _dot(lo, wrh_ref[...]) + _dot(hi, wrl_ref[...]) + br_ref[...]

    lane = lax.broadcasted_iota(jnp.int32, logits.shape, 1)
    lane_f = lane.astype(F32)
    work = logits
    vals, idxs = [], []
    for _ in range(TOP_K):
        top = jnp.max(work, axis=1, keepdims=True)
        first = jnp.min(jnp.where(work == top, lane_f, float(LANES)), axis=1, keepdims=True)
        vals.append(top)
        idxs.append(first)
        work = jnp.where(lane_f == first, -jnp.inf, work)
    exps = [jnp.exp(v - vals[0]) for v in vals]
    denom = exps[0] + exps[1] + exps[2] + exps[3]
    idx_out = jnp.zeros(logits.shape, F32)
    gate_out = jnp.zeros(logits.shape, F32)
    for k in range(TOP_K):
        idx_out = jnp.where(lane == k, idxs[k], idx_out)
        gate_out = jnp.where(lane == k, exps[k] / denom, gate_out)
    idx_ref[...] = idx_out.astype(jnp.int32)
    gate_ref[...] = gate_out


def _mix(xb, x, p, o, wg, wpb, wab, wout, ln_g, ln_b, wr_hi, wr_lo, br):
    T = x.shape[0]
    tm = MIX_ROWS
    row = lambda i: (i, 0)
    fixed = lambda i: (0, 0)
    return pl.pallas_call(
        _mix_kernel,
        grid=(T // tm,),
        in_specs=[
            pl.BlockSpec((tm, D_MODEL), row),
            pl.BlockSpec((tm, D_MODEL), row),
            pl.BlockSpec((tm, POOL_WIDTH), row),
            pl.BlockSpec((tm, V_WIDTH), row),
            pl.BlockSpec(wg.shape, fixed),
            pl.BlockSpec(wpb.shape, fixed),
            pl.BlockSpec(wab.shape, fixed),
            pl.BlockSpec(wout.shape, fixed),
            pl.BlockSpec((1, D_MODEL), fixed),
            pl.BlockSpec((1, D_MODEL), fixed),
            pl.BlockSpec((D_MODEL, LANES), fixed),
            pl.BlockSpec((D_MODEL, LANES), fixed),
            pl.BlockSpec((1, LANES), fixed),
        ],
        out_specs=[
            pl.BlockSpec((tm * TILE_ROWS, LANES), row),
            pl.BlockSpec((tm, LANES), row),
            pl.BlockSpec((tm, LANES), row),
        ],
        out_shape=[
            jax.ShapeDtypeStruct((T * TILE_ROWS, LANES), F32),
            jax.ShapeDtypeStruct((T, LANES), jnp.int32),
            jax.ShapeDtypeStruct((T, LANES), F32),
        ],
        compiler_params=pltpu.CompilerParams(
            dimension_semantics=("parallel",), vmem_limit_bytes=VMEM_LIMIT),
        name="mixer_out_router",
    )(xb, x, p, o, wg, wpb, wab, wout, ln_g, ln_b, wr_hi, wr_lo, br)


def _tile_copy(src, dst, src_row, dst_row, sem):
    return pltpu.make_async_copy(
        src.at[pl.ds(src_row, TILE_ROWS)], dst.at[pl.ds(dst_row, TILE_ROWS)], sem)


def _gmm_kernel(be_ref, nu_ref, src_b0, src_b1, src_n2, dst_prv, x_hbm, wgu_ref, bgu_ref, wd_ref,
                bd_ref, y_hbm, xbuf, ybuf, wgu_b, wd_b, gsem, ssem):
    i = pl.program_id(0)
    n_used = nu_ref[0]
    rows = xbuf.shape[1] // TILE_ROWS
    x_cur, x_n1, x_n2 = i % 3, (i + 1) % 3, (i + 2) % 3
    y_cur = i % 2
    y_prv = 1 - y_cur
    n_chunks = 4
    cw = D_FF // n_chunks
    spare_b = y_hbm.shape[0] - rows * TILE_ROWS

    def wait_gather(slot, count):
        n = count * TILE_ROWS
        pltpu.make_async_copy(x_hbm.at[pl.ds(0, n)], xbuf.at[slot, pl.ds(0, n)],
                              gsem.at[slot]).wait()

    def wait_scatter(slot, count):
        n = count * TILE_ROWS
        pltpu.make_async_copy(ybuf.at[slot, pl.ds(0, n)], y_hbm.at[pl.ds(0, n)],
                              ssem.at[slot]).wait()

    def tile_row(r):
        return r * TILE_ROWS if isinstance(r, int) else pl.multiple_of(r * TILE_ROWS, TILE_ROWS)

    def gather_row(table, slot, r):
        src_row = pl.multiple_of(table[0, 0, r], TILE_ROWS)
        _tile_copy(x_hbm, xbuf.at[slot], src_row, tile_row(r), gsem.at[slot]).start(priority=0)

    def scatter_row(slot, r, dst_row):
        _tile_copy(ybuf.at[slot], y_hbm, tile_row(r), dst_row, ssem.at[slot]).start(priority=1)

    @pl.when(i == 0)
    def _():
        ybuf[...] = jnp.zeros(ybuf.shape, F32)

        def issue(r, c):
            gather_row(src_b0, 0, r)
            gather_row(src_b1, 1, r)
            scatter_row(0, r, spare_b + tile_row(r))
            return c

        lax.fori_loop(0, rows, issue, 0)
        wait_gather(0, rows)

    @pl.when((i < n_used) & ((i == 0) | (be_ref[i] != be_ref[jnp.maximum(i - 1, 0)])))
    def _():
        step = 128

        def cast(c, carry):
            sl = pl.ds(pl.multiple_of(c * step, step), step)
            wgu_b[sl, :] = wgu_ref[sl, :].astype(BF16)
            wd_b[sl, :] = wd_ref[sl, :].astype(BF16)
            return carry

        lax.fori_loop(0, D_MODEL // step, cast, 0)

    @pl.when(i < n_used)
    def _():
        xb = _load_token_tiles(xbuf.at[x_cur], rows).astype(BF16)
        y = jnp.zeros((rows, D_MODEL), F32) + bd_ref[...]
        half = rows // 2
        for c in range(n_chunks):
            if c in (0, 1):
                for r in range(c * half, (c + 1) * half):
                    gather_row(src_n2, x_n2, r)
            if c in (1, 2):
                for r in range((c - 1) * half, c * half):
                    scatter_row(y_prv, r, pl.multiple_of(dst_prv[0, 0, r], TILE_ROWS))
            g_cols = slice(c * cw, (c + 1) * cw)
            u_cols = slice(D_FF + c * cw, D_FF + (c + 1) * cw)
            hg = _dot(xb, wgu_b[:, g_cols]) + bgu_ref[:, g_cols]
            hu = _dot(xb, wgu_b[:, u_cols]) + bgu_ref[:, u_cols]
            hg = jnp.minimum(hg, SWIGLU_LIMIT)
            hu = jnp.clip(hu, -SWIGLU_LIMIT, SWIGLU_LIMIT)
            act = (hu + 1.0) * hg * jax.nn.sigmoid(SWIGLU_ALPHA * hg)
            y = y + _dot(act.astype(BF16), wd_b[g_cols, :])
            if c == 1:
                wait_gather(x_n1, rows)
            if c == 2:
                wait_scatter(y_cur, rows)
        _store_token_tiles(ybuf.at[y_cur], y)

    @pl.when(i == n_used)
    def _():
        def issue(r, c):
            scatter_row(y_prv, r, pl.multiple_of(dst_prv[0, 0, r], TILE_ROWS))
            return c

        lax.fori_loop(0, rows, issue, 0)
        wait_gather(x_n1, rows)
        wait_scatter(y_cur, rows)
        wait_scatter(y_prv, rows)


def _expert_blocks(layer, block_e, n_used, src, dst, x1, w_gu, b_gu, w_down, b_down):
    T = x1.shape[0] // TILE_ROWS
    rows = EXPERT_ROWS
    steps = block_e.shape[0]
    depth = w_gu.shape[0]
    src = src.reshape(steps + 1, 1, rows)
    grid_spec = pltpu.PrefetchScalarGridSpec(
        num_scalar_prefetch=2,
        grid=(steps,),
        in_specs=[
            pl.BlockSpec((1, 1, rows), lambda i, be, nu: (0, 0, 0), memory_space=pltpu.SMEM),
            pl.BlockSpec((1, 1, rows), lambda i, be, nu: (1, 0, 0), memory_space=pltpu.SMEM),
            pl.BlockSpec((1, 1, rows), lambda i, be, nu: (jnp.minimum(i + 2, steps), 0, 0),
                         memory_space=pltpu.SMEM),
            pl.BlockSpec((1, 1, rows), lambda i, be, nu: (i, 0, 0), memory_space=pltpu.SMEM),
            pl.BlockSpec(memory_space=pl.ANY),
            pl.BlockSpec((None, None, D_MODEL, 2 * D_FF), lambda i, be, nu: (layer, be[i], 0, 0)),
            pl.BlockSpec((None, None, 1, 2 * D_FF), lambda i, be, nu: (layer, be[i], 0, 0)),
            pl.BlockSpec((None, None, D_FF, D_MODEL), lambda i, be, nu: (layer, be[i], 0, 0)),
            pl.BlockSpec((None, None, 1, D_MODEL), lambda i, be, nu: (layer, be[i], 0, 0)),
        ],
        out_specs=pl.BlockSpec(memory_space=pl.ANY),
        scratch_shapes=[
            pltpu.VMEM((3, rows * TILE_ROWS, LANES), F32),
            pltpu.VMEM((2, rows * TILE_ROWS, LANES), F32),
            pltpu.VMEM((D_MODEL, 2 * D_FF), BF16),
            pltpu.VMEM((D_FF, D_MODEL), BF16),
            pltpu.SemaphoreType.DMA((3,)),
            pltpu.SemaphoreType.DMA((2,)),
        ],
    )
    return pl.pallas_call(
        _gmm_kernel,
        grid_spec=grid_spec,
        out_shape=jax.ShapeDtypeStruct(((TOP_K * T + 2 * rows) * TILE_ROWS, LANES), F32),
        compiler_params=pltpu.CompilerParams(
            dimension_semantics=("arbitrary",), vmem_limit_bytes=VMEM_LIMIT),
        name="expert_blocks",
    )(block_e, n_used, src, src, src, dst.reshape(steps, 1, rows), x1, w_gu,
      b_gu.reshape(depth, N_EXPERTS, 1, 2 * D_FF), w_down,
      b_down.reshape(depth, N_EXPERTS, 1, D_MODEL))


def _combine_kernel(y0_ref, y1_ref, y2_ref, y3_ref, gate_ref, x1_ref, g_ref, b_ref, x2_ref, xb2_ref):
    tm = x2_ref.shape[0]
    gate = gate_ref[...]
    z = DEEPNORM_ALPHA * _load_token_tiles(x1_ref, tm)
    for k, y_ref in enumerate((y0_ref, y1_ref, y2_ref, y3_ref)):
        z = z + gate[:, k:k + 1] * _load_token_tiles(y_ref, tm)
    x2 = _layer_norm(z, g_ref[...], b_ref[...])
    x2_ref[...] = x2
    xb2_ref[...] = x2.astype(BF16)


def _combine(ys, gate, x1, ln_g, ln_b):
    T = x1.shape[0] // TILE_ROWS
    tm = COMBINE_ROWS
    nt = T // tm
    row = lambda i: (i, 0)
    fixed = lambda i: (0, 0)
    y_specs = [pl.BlockSpec((tm * TILE_ROWS, LANES),
                            functools.partial(lambda k, i: (k * nt + i, 0), k))
               for k in range(TOP_K)]
    return pl.pallas_call(
        _combine_kernel,
        grid=(nt,),
        in_specs=y_specs + [
            pl.BlockSpec((tm, LANES), row),
            pl.BlockSpec((tm * TILE_ROWS, LANES), row),
            pl.BlockSpec((1, D_MODEL), fixed),
            pl.BlockSpec((1, D_MODEL), fixed),
        ],
        out_specs=[
            pl.BlockSpec((tm, D_MODEL), row),
            pl.BlockSpec((tm, D_MODEL), row),
        ],
        out_shape=[
            jax.ShapeDtypeStruct((T, D_MODEL), F32),
            jax.ShapeDtypeStruct((T, D_MODEL), BF16),
        ],
        compiler_params=pltpu.CompilerParams(
            dimension_semantics=("parallel",), vmem_limit_bytes=VMEM_LIMIT),
        name="combine_ln",
    )(ys, ys, ys, ys, gate, x1, ln_g, ln_b)


def _routing_tables(top_idx):
    T = top_idx.shape[0]
    A = T * TOP_K
    rows_per = EXPERT_ROWS
    flat_e = top_idx.reshape(A)
    key = jnp.sort(flat_e * A + jnp.arange(A, dtype=jnp.int32))
    order = key % A
    experts = jnp.arange(N_EXPERTS, dtype=jnp.int32)
    counts = jnp.sum((flat_e[:, None] == experts[None, :]).astype(jnp.int32), axis=0)
    padded = (counts + rows_per - 1) // rows_per * rows_per
    start = jnp.cumsum(counts) - counts
    pend = jnp.cumsum(padded)
    pstart = pend - padded
    steps = A // rows_per + N_EXPERTS + 1
    first_row = jnp.arange(steps + 1, dtype=jnp.int32) * rows_per
    block_e = jnp.minimum(
        jnp.sum((pend[None, :] <= first_row[:, None]).astype(jnp.int32), axis=1), N_EXPERTS - 1)
    r = jnp.arange((steps + 1) * rows_per, dtype=jnp.int32)
    e_r = jnp.repeat(block_e, rows_per)
    off = r - pstart[e_r]
    valid = off < counts[e_r]
    a = order[jnp.clip(start[e_r] + off, 0, A - 1)]
    tok = a // TOP_K
    src = jnp.where(valid, tok, 0).astype(jnp.int32) * TILE_ROWS
    spare = TOP_K * T + r % rows_per
    dst = jnp.where(valid, (a % TOP_K) * T + tok, spare).astype(jnp.int32)
    dst = jnp.concatenate([spare[:rows_per], dst[:-2 * rows_per]]).astype(jnp.int32) * TILE_ROWS
    n_used = (pend[-1] // rows_per).astype(jnp.int32).reshape(1)
    return src, dst, block_e[:steps].astype(jnp.int32), n_used


def _rotary_tables(positions):
    half = ROT_DIM // 2
    inv_freq = ROPE_THETA ** (-jnp.arange(0, ROT_DIM, 2, dtype=F32) / ROT_DIM)
    ang = positions.reshape(-1).astype(F32)[:, None] * inv_freq
    cos, sin = jnp.cos(ang), jnp.sin(ang)
    T = ang.shape[0]
    ones = jnp.ones((T, HEAD_DIM - ROT_DIM), F32)
    zeros = jnp.zeros((T, HEAD_DIM - ROT_DIM), F32)
    zh = jnp.zeros((T, half), F32)
    c = jnp.concatenate([cos, cos, ones], axis=1)
    sa = jnp.concatenate([-sin, zh, zeros], axis=1)
    sb = jnp.concatenate([zh, sin, zeros], axis=1)
    reps = LANES // HEAD_DIM
    return jnp.tile(c, (1, reps)), jnp.tile(sa, (1, reps)), jnp.tile(sb, (1, reps))


def kernel(x, positions, w_in, pool_w, pool_scale, w_pool_branch, w_attn_branch, lambda_q1, lambda_k1, lambda_q2, lambda_k2, subln_w, w_out, ln1_g, ln1_b, w_router, b_router, w_gu, b_gu, w_down, b_down, ln2_g, ln2_b):
    B, S, D = x.shape
    assert D == D_MODEL and S % ATTN_BLOCK == 0 and w_in.shape[0] == DEPTH
    T = B * S
    assert T % PROJ_ROWS == 0 and T % MIX_ROWS == 0 and T % COMBINE_ROWS == 0
    assert (T * TOP_K) % EXPERT_ROWS == 0
    rot_c, rot_sa, rot_sb = _rotary_tables(positions)
    xf = x.reshape(T, D)
    xb = xf.astype(BF16)
    o_qk = POOL_WIDTH
    o_v = o_qk + 2 * QK_WIDTH
    o_g = o_v + V_WIDTH
    for l in range(DEPTH):
        wl = w_in[l]
        wu = wl[:, :o_qk].astype(BF16)
        wq = wl[:, o_qk:o_qk + QK_WIDTH] * (HEAD_DIM ** -0.5)
        wqk = jnp.concatenate([wq, wl[:, o_qk + QK_WIDTH:o_v]], axis=1).astype(BF16)
        wv = wl[:, o_v:o_g].astype(BF16)
        wg = wl[:, o_g:].astype(BF16)
        u, qk, v = _projections(xb, wu, wqk, wv, rot_c, rot_sa, rot_sb)
        p = _pool(u, pool_w[l].astype(BF16), pool_scale[l].reshape(1, POOL_WIDTH), B, S)
        lambda_init = 0.8 - 0.6 * math.exp(-0.3 * l)
        lam = (jnp.exp(jnp.sum(lambda_q1[l] * lambda_k1[l]))
               - jnp.exp(jnp.sum(lambda_q2[l] * lambda_k2[l])) + lambda_init).reshape(1)
        o = _attention(qk, v, lam, subln_w[l].reshape(1, V_HEAD_DIM), B, S, 1.0 - lambda_init)
        wr = jnp.zeros((D, LANES), F32).at[:, :N_EXPERTS].set(w_router[l])
        wr_hi = wr.astype(BF16)
        wr_lo = (wr - wr_hi.astype(F32)).astype(BF16)
        br = jnp.full((1, LANES), NEG_BIG, F32).at[0, :N_EXPERTS].set(b_router[l])
        x1, idx, gate = _mix(
            xb, xf, p, o, wg, w_pool_branch[l].astype(BF16), w_attn_branch[l].astype(BF16),
            w_out[l].astype(BF16), ln1_g[l].reshape(1, D), ln1_b[l].reshape(1, D),
            wr_hi, wr_lo, br)
        src, dst, block_e, n_used = _routing_tables(idx[:, :TOP_K])
        ys = _expert_blocks(l, block_e, n_used, src, dst, x1, w_gu, b_gu, w_down, b_down)
        xf, xb = _combine(ys, gate, x1, ln2_g[l].reshape(1, D), ln2_b[l].reshape(1, D))
    return xf.reshape(B, S, D)
```

```python
import functools
import math

import jax
import jax.numpy as jnp
from jax import lax
from jax.experimental import pallas as pl
from jax.experimental.pallas import tpu as pltpu

F32 = jnp.float32
BF16 = jnp.bfloat16

D_MODEL = 1024
DEPTH = 2
POOL_WINDOWS = (2, 4, 8, 16)
POOL_GROUP_DIM = 128
POOL_WIDTH = POOL_GROUP_DIM * len(POOL_WINDOWS)
N_HEADS = 8
HEAD_DIM = 64
V_HEAD_DIM = 2 * HEAD_DIM
QK_WIDTH = N_HEADS * 2 * HEAD_DIM
V_WIDTH = N_HEADS * V_HEAD_DIM
ROPE_THETA = 500000.0
ROT_DIM = HEAD_DIM // 4
N_EXPERTS = 32
TOP_K = 4
D_FF = D_MODEL
SWIGLU_ALPHA = 1.702
SWIGLU_LIMIT = 7.0
LN_EPS = 1e-5
DEEPNORM_ALPHA = (2 * DEPTH) ** 0.25

LANES = 128
VMEM_LIMIT = 48 * 1024 * 1024

PROJ_ROWS = 512
ATTN_BLOCK = 512
MIX_ROWS = 256
EXPERT_ROWS = 256
COMBINE_ROWS = 256
NEG_BIG = -1e30


def _dot(a, b):
    return jnp.dot(a, b, preferred_element_type=F32)


def _layer_norm(z, g, b):
    mu = jnp.mean(z, axis=-1, keepdims=True)
    zc = z - mu
    var = jnp.mean(zc * zc, axis=-1, keepdims=True)
    return zc * lax.rsqrt(var + LN_EPS) * g + b


TILE_ROWS = D_MODEL // LANES


def _load_token_tiles(ref, tokens):
    return jnp.concatenate(
        [ref[pl.ds(s, tokens, stride=TILE_ROWS), :] for s in range(TILE_ROWS)], axis=1)


def _store_token_tiles(ref, value):
    tokens = value.shape[0]
    for s in range(TILE_ROWS):
        ref[pl.ds(s, tokens, stride=TILE_ROWS), :] = value[:, s * LANES:(s + 1) * LANES]


def _proj_kernel(xb_ref, wu_ref, wqk_ref, wv_ref, c_ref, sa_ref, sb_ref, u_ref, qk_ref, v_ref):
    xb = xb_ref[...]
    u_ref[...] = _dot(xb, wu_ref[...])
    c = c_ref[...]
    sa = sa_ref[...]
    sb = sb_ref[...]
    chunk = 4 * LANES
    for j in range(2 * QK_WIDTH // chunk):
        t = _dot(xb, wqk_ref[:, j * chunk:(j + 1) * chunk])
        for s in range(chunk // LANES):
            ts = t[:, s * LANES:(s + 1) * LANES]
            up = pltpu.roll(ts, LANES - ROT_DIM // 2, axis=1)
            dn = pltpu.roll(ts, ROT_DIM // 2, axis=1)
            lo = j * chunk + s * LANES
            qk_ref[:, lo:lo + LANES] = (ts * c + up * sa + dn * sb).astype(BF16)
    for j in range(V_WIDTH // chunk):
        v_ref[:, j * chunk:(j + 1) * chunk] = _dot(
            xb, wv_ref[:, j * chunk:(j + 1) * chunk]).astype(BF16)


def _projections(xb, wu, wqk, wv, rot_c, rot_sa, rot_sb):
    T = xb.shape[0]
    tm = PROJ_ROWS
    row = lambda i: (i, 0)
    fixed = lambda i: (0, 0)
    return pl.pallas_call(
        _proj_kernel,
        grid=(T // tm,),
        in_specs=[
            pl.BlockSpec((tm, D_MODEL), row),
            pl.BlockSpec((D_MODEL, POOL_WIDTH), fixed),
            pl.BlockSpec((D_MODEL, 2 * QK_WIDTH), fixed),
            pl.BlockSpec((D_MODEL, V_WIDTH), fixed),
            pl.BlockSpec((tm, LANES), row),
            pl.BlockSpec((tm, LANES), row),
            pl.BlockSpec((tm, LANES), row),
        ],
        out_specs=[
            pl.BlockSpec((tm, POOL_WIDTH), row),
            pl.BlockSpec((tm, 2 * QK_WIDTH), row),
            pl.BlockSpec((tm, V_WIDTH), row),
        ],
        out_shape=[
            jax.ShapeDtypeStruct((T, POOL_WIDTH), F32),
            jax.ShapeDtypeStruct((T, 2 * QK_WIDTH), BF16),
            jax.ShapeDtypeStruct((T, V_WIDTH), BF16),
        ],
        compiler_params=pltpu.CompilerParams(
            dimension_semantics=("parallel",), vmem_limit_bytes=VMEM_LIMIT),
        name="projections",
    )(xb, wu, wqk, wv, rot_c, rot_sa, rot_sb)


def _pool_kernel(u_ref, w_ref, sc_ref, p_ref):
    S = u_ref.shape[0]
    row = lax.broadcasted_iota(jnp.int32, (S, POOL_GROUP_DIM), 0)
    for g, window in enumerate(POOL_WINDOWS):
        cols = slice(g * POOL_GROUP_DIM, (g + 1) * POOL_GROUP_DIM)
        u = u_ref[:, cols]
        acc = u
        span = 1
        while span < window:
            shifted = jnp.where(row >= span, pltpu.roll(acc, span, axis=0), 0.0)
            acc = acc + shifted
            span *= 2
        count = jnp.minimum(row + 1, window).astype(F32)
        d = (acc / count - u).astype(BF16)
        y = _dot(d, w_ref[g]) * sc_ref[:, cols]
        p_ref[:, cols] = y.astype(BF16)


def _pool(u, pool_w, pool_scale, B, S):
    T = u.shape[0]
    return pl.pallas_call(
        _pool_kernel,
        grid=(B,),
        in_specs=[
            pl.BlockSpec((S, POOL_WIDTH), lambda b: (b, 0)),
            pl.BlockSpec(pool_w.shape, lambda b: (0, 0, 0)),
            pl.BlockSpec((1, POOL_WIDTH), lambda b: (0, 0)),
        ],
        out_specs=pl.BlockSpec((S, POOL_WIDTH), lambda b: (b, 0)),
        out_shape=jax.ShapeDtypeStruct((T, POOL_WIDTH), BF16),
        compiler_params=pltpu.CompilerParams(
            dimension_semantics=("parallel",), vmem_limit_bytes=VMEM_LIMIT),
        name="pool",
    )(u, pool_w, pool_scale)


def _attn_kernel(lam_ref, q_ref, k_ref, v_ref, w_ref, o_ref, m_ref, l_ref, acc_ref, *,
                 post_scale, tq):
    nq = q_ref.shape[0] // tq
    lane = lax.broadcasted_iota(jnp.int32, (tq, V_HEAD_DIM), 1)
    m_ref[...] = jnp.full(m_ref.shape, -jnp.inf, F32)
    l_ref[...] = jnp.zeros(l_ref.shape, F32)
    acc_ref[...] = jnp.zeros(acc_ref.shape, F32)

    def stacked_queries(qi):
        q = q_ref[qi * tq:(qi + 1) * tq, :]
        zero = jnp.zeros_like(q)
        return jnp.concatenate(
            [jnp.where(lane < HEAD_DIM, q, zero), jnp.where(lane >= HEAD_DIM, q, zero)], axis=0)

    for j in range(nq):
        ks = k_ref[j * tq:(j + 1) * tq, :]
        vt = v_ref[j * tq:(j + 1) * tq, :].astype(F32).T.astype(BF16)
        for qi in range(j, nq):
            s = lax.dot_general(ks, stacked_queries(qi), (((1,), (1,)), ((), ())),
                                preferred_element_type=F32)
            if qi == j:
                key = lax.broadcasted_iota(jnp.int32, s.shape, 0)
                qry = lax.broadcasted_iota(jnp.int32, s.shape, 1)
                qry = jnp.where(qry >= tq, qry - tq, qry)
                s = jnp.where(key <= qry, s, -jnp.inf)
            m = m_ref[qi]
            m_new = jnp.maximum(m, jnp.max(s, axis=0, keepdims=True))
            rescale = jnp.exp2(m - m_new)
            p = jnp.exp2(s - m_new)
            m_ref[qi] = m_new
            l_ref[qi] = rescale * l_ref[qi] + jnp.sum(p, axis=0, keepdims=True)
            acc_ref[qi] = rescale * acc_ref[qi] + _dot(vt, p.astype(BF16))
        o = acc_ref[j] / l_ref[j]
        od = (o[:, :tq] - lam_ref[0] * o[:, tq:]).T
        od = od * lax.rsqrt(jnp.mean(od * od, axis=-1, keepdims=True) + LN_EPS)
        o_ref[j * tq:(j + 1) * tq, :] = (od * w_ref[...] * post_scale).astype(BF16)


def _attention(qk, v, lam, subln_w, B, S, post_scale):
    T = qk.shape[0]
    tq = ATTN_BLOCK
    nq = S // tq
    head = lambda b, h: (b, h)
    return pl.pallas_call(
        functools.partial(_attn_kernel, post_scale=post_scale, tq=tq),
        grid=(B, N_HEADS),
        in_specs=[
            pl.BlockSpec(memory_space=pltpu.SMEM),
            pl.BlockSpec((S, V_HEAD_DIM), head),
            pl.BlockSpec((S, V_HEAD_DIM), lambda b, h: (b, N_HEADS + h)),
            pl.BlockSpec((S, V_HEAD_DIM), head),
            pl.BlockSpec((1, V_HEAD_DIM), lambda b, h: (0, 0)),
        ],
        out_specs=pl.BlockSpec((S, V_HEAD_DIM), head),
        out_shape=jax.ShapeDtypeStruct((T, V_WIDTH), BF16),
        scratch_shapes=[
            pltpu.VMEM((nq, 1, 2 * tq), F32),
            pltpu.VMEM((nq, 1, 2 * tq), F32),
            pltpu.VMEM((nq, V_HEAD_DIM, 2 * tq), F32),
        ],
        compiler_params=pltpu.CompilerParams(
            dimension_semantics=("parallel", "parallel"), vmem_limit_bytes=VMEM_LIMIT),
        name="diff_attention",
    )(lam, qk, qk, v, subln_w)


def _mix_kernel(xb_ref, x_ref, p_ref, o_ref, wg_ref, wpb_ref, wab_ref, wout_ref, g_ref, b_ref,
                wrh_ref, wrl_ref, br_ref, x1_ref, idx_ref, gate_ref, z_ref):
    @pl.when(pl.program_id(0) == 0)
    def _():
        z_ref[...] = jnp.zeros(z_ref.shape, F32)

    x1 = _layer_norm(z_ref[...], g_ref[...], b_ref[...])
    _store_token_tiles(x1_ref, x1)

    hi = x1.astype(BF16)
    lo = (x1 - hi.astype(F32)).astype(BF16)
    logits = _dot(hi, wrh_ref[...]) + _dot(lo, wrh_ref[...]) + _dot(hi, wrl_ref[...]) + br_ref[...]

    lane = lax.broadcasted_iota(jnp.int32, logits.shape, 1)
    lane_f = lane.astype(F32)
    work = logits
    vals, idxs = [], []
    for _ in range(TOP_K):
        top = jnp.max(work, axis=1, keepdims=True)
        first = jnp.min(jnp.where(work == top, lane_f, float(LANES)), axis=1, keepdims=True)
        vals.append(top)
        idxs.append(first)
        work = jnp.where(lane_f == first, -jnp.inf, work)
    exps = [jnp.exp(v - vals[0]) for v in vals]
    denom = exps[0] + exps[1] + exps[2] + exps[3]
    idx_out = jnp.zeros(logits.shape, F32)
    gate_out = jnp.zeros(logits.shape, F32)
    for k in range(TOP_K):
        idx_out = jnp.where(lane == k, idxs[k], idx_out)
        gate_out = jnp.where(lane == k, exps[k] / denom, gate_out)
    idx_ref[...] = idx_out.astype(jnp.int32)
    gate_ref[...] = gate_out

    xb = xb_ref[...]
    merged = jax.nn.sigmoid(_dot(xb, wg_ref[:, :D_MODEL])) * _dot(p_ref[...], wpb_ref[...])
    merged = merged + jax.nn.sigmoid(_dot(xb, wg_ref[:, D_MODEL:])) * _dot(o_ref[...], wab_ref[...])
    z_ref[...] = DEEPNORM_ALPHA * x_ref[...] + _dot(merged.astype(BF16), wout_ref[...])


def _mix(xb, x, p, o, wg, wpb, wab, wout, ln_g, ln_b, wr_hi, wr_lo, br):
    T = x.shape[0]
    tm = MIX_ROWS
    nt = T // tm
    row = lambda i: (jnp.minimum(i, nt - 1), 0)
    out_row = lambda i: (jnp.maximum(i - 1, 0), 0)
    fixed = lambda i: (0, 0)
    return pl.pallas_call(
        _mix_kernel,
        grid=(nt + 1,),
        in_specs=[
            pl.BlockSpec((tm, D_MODEL), row),
            pl.BlockSpec((tm, D_MODEL), row),
            pl.BlockSpec((tm, POOL_WIDTH), row),
            pl.BlockSpec((tm, V_WIDTH), row),
            pl.BlockSpec(wg.shape, fixed),
            pl.BlockSpec(wpb.shape, fixed),
            pl.BlockSpec(wab.shape, fixed),
            pl.BlockSpec(wout.shape, fixed),
            pl.BlockSpec((1, D_MODEL), fixed),
            pl.BlockSpec((1, D_MODEL), fixed),
            pl.BlockSpec((D_MODEL, LANES), fixed),
            pl.BlockSpec((D_MODEL, LANES), fixed),
            pl.BlockSpec((1, LANES), fixed),
        ],
        out_specs=[
            pl.BlockSpec((tm * TILE_ROWS, LANES), out_row),
            pl.BlockSpec((tm, LANES), out_row),
            pl.BlockSpec((tm, LANES), out_row),
        ],
        out_shape=[
            jax.ShapeDtypeStruct((T * TILE_ROWS, LANES), F32),
            jax.ShapeDtypeStruct((T, LANES), jnp.int32),
            jax.ShapeDtypeStruct((T, LANES), F32),
        ],
        scratch_shapes=[pltpu.VMEM((tm, D_MODEL), F32)],
        compiler_params=pltpu.CompilerParams(
            dimension_semantics=("arbitrary",), vmem_limit_bytes=VMEM_LIMIT),
        name="mixer_out_router",
    )(xb, x, p, o, wg, wpb, wab, wout, ln_g, ln_b, wr_hi, wr_lo, br)


def _tile_copy(src, dst, src_row, dst_row, sem):
    return pltpu.make_async_copy(
        src.at[pl.ds(src_row, TILE_ROWS)], dst.at[pl.ds(dst_row, TILE_ROWS)], sem)


def _gmm_kernel(be_ref, nu_ref, src_b0, src_b1, src_n2, dst_prv, x_hbm, wgu_ref, bgu_ref, wd_ref,
                bd_ref, y_hbm, xbuf, ybuf, wgu_b, wd_b, gsem, ssem):
    i = pl.program_id(0)
    n_used = nu_ref[0]
    rows = xbuf.shape[1] // TILE_ROWS
    x_cur, x_n1, x_n2 = i % 3, (i + 1) % 3, (i + 2) % 3
    y_cur = i % 2
    y_prv = 1 - y_cur
    n_chunks = 4
    cw = D_FF // n_chunks
    spare_b = y_hbm.shape[0] - rows * TILE_ROWS

    def wait_gather(slot, count):
        n = count * TILE_ROWS
        pltpu.make_async_copy(x_hbm.at[pl.ds(0, n)], xbuf.at[slot, pl.ds(0, n)],
                              gsem.at[slot]).wait()

    def wait_scatter(slot, count):
        n = count * TILE_ROWS
        pltpu.make_async_copy(ybuf.at[slot, pl.ds(0, n)], y_hbm.at[pl.ds(0, n)],
                              ssem.at[slot]).wait()

    def tile_row(r):
        return r * TILE_ROWS if isinstance(r, int) else pl.multiple_of(r * TILE_ROWS, TILE_ROWS)

    def gather_row(table, slot, r):
        src_row = pl.multiple_of(table[0, 0, r], TILE_ROWS)
        _tile_copy(x_hbm, xbuf.at[slot], src_row, tile_row(r), gsem.at[slot]).start(priority=0)

    def scatter_row(slot, r, dst_row):
        _tile_copy(ybuf.at[slot], y_hbm, tile_row(r), dst_row, ssem.at[slot]).start(priority=1)

    @pl.when(i == 0)
    def _():
        ybuf[...] = jnp.zeros(ybuf.shape, F32)

        def issue(r, c):
            gather_row(src_b0, 0, r)
            gather_row(src_b1, 1, r)
            scatter_row(0, r, spare_b + tile_row(r))
            return c

        lax.fori_loop(0, rows, issue, 0)
        wait_gather(0, rows)

    @pl.when((i < n_used) & ((i == 0) | (be_ref[i] != be_ref[jnp.maximum(i - 1, 0)])))
    def _():
        step = 128

        def cast(c, carry):
            sl = pl.ds(pl.multiple_of(c * step, step), step)
            wgu_b[sl, :] = wgu_ref[sl, :].astype(BF16)
            wd_b[sl, :] = wd_ref[sl, :].astype(BF16)
            return carry

        lax.fori_loop(0, D_MODEL // step, cast, 0)

    @pl.when(i < n_used)
    def _():
        xb = _load_token_tiles(xbuf.at[x_cur], rows).astype(BF16)
        y = jnp.zeros((rows, D_MODEL), F32) + bd_ref[...]
        half = rows // 2
        for c in range(n_chunks):
            if c in (0, 1):
                for r in range(c * half, (c + 1) * half):
                    gather_row(src_n2, x_n2, r)
            if c in (1, 2):
                for r in range((c - 1) * half, c * half):
                    scatter_row(y_prv, r, pl.multiple_of(dst_prv[0, 0, r], TILE_ROWS))
            g_cols = slice(c * cw, (c + 1) * cw)
            u_cols = slice(D_FF + c * cw, D_FF + (c + 1) * cw)
            hg = _dot(xb, wgu_b[:, g_cols]) + bgu_ref[:, g_cols]
            hu = _dot(xb, wgu_b[:, u_cols]) + bgu_ref[:, u_cols]
            hg = jnp.minimum(hg, SWIGLU_LIMIT)
            hu = jnp.clip(hu, -SWIGLU_LIMIT, SWIGLU_LIMIT)
            act = (hu + 1.0) * hg * jax.nn.sigmoid(SWIGLU_ALPHA * hg)
            y = y + _dot(act.astype(BF16), wd_b[g_cols, :])
            if c == 1:
                wait_gather(x_n1, rows)
            if c == 2:
                wait_scatter(y_cur, rows)
        _store_token_tiles(ybuf.at[y_cur], y)

    @pl.when(i == n_used)
    def _():
        def issue(r, c):
            scatter_row(y_prv, r, pl.multiple_of(dst_prv[0, 0, r], TILE_ROWS))
            return c

        lax.fori_loop(0, rows, issue, 0)
        wait_gather(x_n1, rows)
        wait_scatter(y_cur, rows)
        wait_scatter(y_prv, rows)


def _expert_blocks(layer, block_e, n_used, src, dst, x1, w_gu, b_gu, w_down, b_down):
    T = x1.shape[0] // TILE_ROWS
    rows = EXPERT_ROWS
    steps = block_e.shape[0]
    depth = w_gu.shape[0]
    src = src.reshape(steps + 1, 1, rows)
    grid_spec = pltpu.PrefetchScalarGridSpec(
        num_scalar_prefetch=2,
        grid=(steps,),
        in_specs=[
            pl.BlockSpec((1, 1, rows), lambda i, be, nu: (0, 0, 0), memory_space=pltpu.SMEM),
            pl.BlockSpec((1, 1, rows), lambda i, be, nu: (1, 0, 0), memory_space=pltpu.SMEM),
            pl.BlockSpec((1, 1, rows), lambda i, be, nu: (jnp.minimum(i + 2, steps), 0, 0),
                         memory_space=pltpu.SMEM),
            pl.BlockSpec((1, 1, rows), lambda i, be, nu: (i, 0, 0), memory_space=pltpu.SMEM),
            pl.BlockSpec(memory_space=pl.ANY),
            pl.BlockSpec((None, None, D_MODEL, 2 * D_FF), lambda i, be, nu: (layer, be[i], 0, 0)),
            pl.BlockSpec((None, None, 1, 2 * D_FF), lambda i, be, nu: (layer, be[i], 0, 0)),
            pl.BlockSpec((None, None, D_FF, D_MODEL), lambda i, be, nu: (layer, be[i], 0, 0)),
            pl.BlockSpec((None, None, 1, D_MODEL), lambda i, be, nu: (layer, be[i], 0, 0)),
        ],
        out_specs=pl.BlockSpec(memory_space=pl.ANY),
        scratch_shapes=[
            pltpu.VMEM((3, rows * TILE_ROWS, LANES), F32),
            pltpu.VMEM((2, rows * TILE_ROWS, LANES), F32),
            pltpu.VMEM((D_MODEL, 2 * D_FF), BF16),
            pltpu.VMEM((D_FF, D_MODEL), BF16),
            pltpu.SemaphoreType.DMA((3,)),
            pltpu.SemaphoreType.DMA((2,)),
        ],
    )
    return pl.pallas_call(
        _gmm_kernel,
        grid_spec=grid_spec,
        out_shape=jax.ShapeDtypeStruct(((TOP_K * T + 2 * rows) * TILE_ROWS, LANES), F32),
        compiler_params=pltpu.CompilerParams(
            dimension_semantics=("arbitrary",), vmem_limit_bytes=VMEM_LIMIT),
        name="expert_blocks",
    )(block_e, n_used, src, src, src, dst.reshape(steps, 1, rows), x1, w_gu,
      b_gu.reshape(depth, N_EXPERTS, 1, 2 * D_FF), w_down,
      b_down.reshape(depth, N_EXPERTS, 1, D_MODEL))


def _combine_kernel(y0_ref, y1_ref, y2_ref, y3_ref, gate_ref, x1_ref, g_ref, b_ref, x2_ref, xb2_ref):
    tm = x2_ref.shape[0]
    gate = gate_ref[...]
    z = DEEPNORM_ALPHA * _load_token_tiles(x1_ref, tm)
    for k, y_ref in enumerate((y0_ref, y1_ref, y2_ref, y3_ref)):
        z = z + gate[:, k:k + 1] * _load_token_tiles(y_ref, tm)
    x2 = _layer_norm(z, g_ref[...], b_ref[...])
    x2_ref[...] = x2
    xb2_ref[...] = x2.astype(BF16)


def _combine(ys, gate, x1, ln_g, ln_b):
    T = x1.shape[0] // TILE_ROWS
    tm = COMBINE_ROWS
    nt = T // tm
    row = lambda i: (i, 0)
    fixed = lambda i: (0, 0)
    y_specs = [pl.BlockSpec((tm * TILE_ROWS, LANES),
                            functools.partial(lambda k, i: (k * nt + i, 0), k))
               for k in range(TOP_K)]
    return pl.pallas_call(
        _combine_kernel,
        grid=(nt,),
        in_specs=y_specs + [
            pl.BlockSpec((tm, LANES), row),
            pl.BlockSpec((tm * TILE_ROWS, LANES), row),
            pl.BlockSpec((1, D_MODEL), fixed),
            pl.BlockSpec((1, D_MODEL), fixed),
        ],
        out_specs=[
            pl.BlockSpec((tm, D_MODEL), row),
            pl.BlockSpec((tm, D_MODEL), row),
        ],
        out_shape=[
            jax.ShapeDtypeStruct((T, D_MODEL), F32),
            jax.ShapeDtypeStruct((T, D_MODEL), BF16),
        ],
        compiler_params=pltpu.CompilerParams(
            dimension_semantics=("parallel",), vmem_limit_bytes=VMEM_LIMIT),
        name="combine_ln",
    )(ys, ys, ys, ys, gate, x1, ln_g, ln_b)


def _routing_tables(top_idx):
    T = top_idx.shape[0]
    A = T * TOP_K
    rows_per = EXPERT_ROWS
    flat_e = top_idx.reshape(A)
    key = jnp.sort(flat_e * A + jnp.arange(A, dtype=jnp.int32))
    order = key % A
    experts = jnp.arange(N_EXPERTS, dtype=jnp.int32)
    counts = jnp.sum((flat_e[:, None] == experts[None, :]).astype(jnp.int32), axis=0)
    padded = (counts + rows_per - 1) // rows_per * rows_per
    start = jnp.cumsum(counts) - counts
    pend = jnp.cumsum(padded)
    pstart = pend - padded
    steps = A // rows_per + N_EXPERTS + 1
    first_row = jnp.arange(steps + 1, dtype=jnp.int32) * rows_per
    block_e = jnp.minimum(
        jnp.sum((pend[None, :] <= first_row[:, None]).astype(jnp.int32), axis=1), N_EXPERTS - 1)
    r = jnp.arange((steps + 1) * rows_per, dtype=jnp.int32)
    e_r = jnp.repeat(block_e, rows_per)
    off = r - pstart[e_r]
    valid = off < counts[e_r]
    a = order[jnp.clip(start[e_r] + off, 0, A - 1)]
    tok = a // TOP_K
    src = jnp.where(valid, tok, 0).astype(jnp.int32) * TILE_ROWS
    spare = TOP_K * T + r % rows_per
    dst = jnp.where(valid, (a % TOP_K) * T + tok, spare).astype(jnp.int32)
    dst = jnp.concatenate([spare[:rows_per], dst[:-2 * rows_per]]).astype(jnp.int32) * TILE_ROWS
    n_used = (pend[-1] // rows_per).astype(jnp.int32).reshape(1)
    return src, dst, block_e[:steps].astype(jnp.int32), n_used


def _rotary_tables(positions):
    half = ROT_DIM // 2
    inv_freq = ROPE_THETA ** (-jnp.arange(0, ROT_DIM, 2, dtype=F32) / ROT_DIM)
    ang = positions.reshape(-1).astype(F32)[:, None] * inv_freq
    cos, sin = jnp.cos(ang), jnp.sin(ang)
    T = ang.shape[0]
    ones = jnp.ones((T, HEAD_DIM - ROT_DIM), F32)
    zeros = jnp.zeros((T, HEAD_DIM - ROT_DIM), F32)
    zh = jnp.zeros((T, half), F32)
    c = jnp.concatenate([cos, cos, ones], axis=1)
    sa = jnp.concatenate([-sin, zh, zeros], axis=1)
    sb = jnp.concatenate([zh, sin, zeros], axis=1)
    reps = LANES // HEAD_DIM
    return jnp.tile(c, (1, reps)), jnp.tile(sa, (1, reps)), jnp.tile(sb, (1, reps))


def kernel(x, positions, w_in, pool_w, pool_scale, w_pool_branch, w_attn_branch, lambda_q1, lambda_k1, lambda_q2, lambda_k2, subln_w, w_out, ln1_g, ln1_b, w_router, b_router, w_gu, b_gu, w_down, b_down, ln2_g, ln2_b):
    B, S, D = x.shape
    assert D == D_MODEL and S % ATTN_BLOCK == 0 and w_in.shape[0] == DEPTH
    T = B * S
    assert T % PROJ_ROWS == 0 and T % MIX_ROWS == 0 and T % COMBINE_ROWS == 0
    assert (T * TOP_K) % EXPERT_ROWS == 0
    rot_c, rot_sa, rot_sb = _rotary_tables(positions)
    xf = x.reshape(T, D)
    xb = xf.astype(BF16)
    o_qk = POOL_WIDTH
    o_v = o_qk + 2 * QK_WIDTH
    o_g = o_v + V_WIDTH
    for l in range(DEPTH):
        wl = w_in[l]
        wu = wl[:, :o_qk].astype(BF16)
        wq = wl[:, o_qk:o_qk + QK_WIDTH] * (HEAD_DIM ** -0.5 * math.log2(math.e))
        wqk = jnp.concatenate([wq, wl[:, o_qk + QK_WIDTH:o_v]], axis=1).astype(BF16)
        wv = wl[:, o_v:o_g].astype(BF16)
        wg = wl[:, o_g:].astype(BF16)
        u, qk, v = _projections(xb, wu, wqk, wv, rot_c, rot_sa, rot_sb)
        p = _pool(u, pool_w[l].astype(BF16), pool_scale[l].reshape(1, POOL_WIDTH), B, S)
        lambda_init = 0.8 - 0.6 * math.exp(-0.3 * l)
        lam = (jnp.exp(jnp.sum(lambda_q1[l] * lambda_k1[l]))
               - jnp.exp(jnp.sum(lambda_q2[l] * lambda_k2[l])) + lambda_init).reshape(1)
        o = _attention(qk, v, lam, subln_w[l].reshape(1, V_HEAD_DIM), B, S, 1.0 - lambda_init)
        wr = jnp.zeros((D, LANES), F32).at[:, :N_EXPERTS].set(w_router[l])
        wr_hi = wr.astype(BF16)
        wr_lo = (wr - wr_hi.astype(F32)).astype(BF16)
        br = jnp.full((1, LANES), NEG_BIG, F32).at[0, :N_EXPERTS].set(b_router[l])
        x1, idx, gate = _mix(
            xb, xf, p, o, wg, w_pool_branch[l].astype(BF16), w_attn_branch[l].astype(BF16),
            w_out[l].astype(BF16), ln1_g[l].reshape(1, D), ln1_b[l].reshape(1, D),
            wr_hi, wr_lo, br)
        src, dst, block_e, n_used = _routing_tables(idx[:, :TOP_K])
        ys = _expert_blocks(l, block_e, n_used, src, dst, x1, w_gu, b_gu, w_down, b_down)
        xf, xb = _combine(ys, gate, x1, ln2_g[l].reshape(1, D), ln2_b[l].reshape(1, D))
    return xf.reshape(B, S, D)
```
